```python
import math
import jax, jax.numpy as jnp
from jax import lax
import numpy as np

D_MODEL = 2048
BATCH = 4
SEQ = 2048
DEPTH = 4
DEC_BATCH = 128
DEC_SEQ = 8
PAST_LEN = 16384
PAGE_SIZE = 128

N_MIXERS = 2
N_POOL_LAYERS = (DEPTH + 1) // 2
N_MLSTM_LAYERS = DEPTH // 2
POOL_WINDOWS = (2, 4, 8, 16)
N_POOL_GROUPS = len(POOL_WINDOWS)
POOL_GROUP_DIM = D_MODEL // N_POOL_GROUPS
POOL_BUF = max(POOL_WINDOWS) - 1
MLSTM_HEADS = 4
QK_DIM = D_MODEL // 2
V_DIM = D_MODEL
QK_HEAD_DIM = QK_DIM // MLSTM_HEADS
V_HEAD_DIM = V_DIM // MLSTM_HEADS
MLSTM_IN_DIM = 2 * QK_DIM + 2 * V_DIM + 2 * MLSTM_HEADS
CHUNK = 64
GATE_CAP = 15.0
D_FF = 4 * D_MODEL
EPS = 1e-6

kernel_name = "hybrid_pool_mlstm_decoder_step"


def rmsnorm(x, g):
    xf = x.astype(jnp.float32)
    y = xf * lax.rsqrt(jnp.mean(xf * xf, axis=-1, keepdims=True) + EPS)
    return (y * g.astype(jnp.float32)).astype(x.dtype)


def sq_relu_mlp(u, w_up, w_down):
    a = jax.nn.relu(u @ w_up)
    return (a * a) @ w_down


def pool_mixer(u, buf, pos0, w_group, scale):
    B, T, D = u.shape
    P = POOL_BUF
    u_ext = jnp.concatenate([buf.astype(u.dtype), u], axis=1).astype(jnp.float32)
    cs = jnp.cumsum(jnp.pad(u_ext, ((0, 0), (1, 0), (0, 0))), axis=1)
    pos = pos0 + jnp.arange(T, dtype=jnp.int32)
    outs = []
    for g, w in enumerate(POOL_WINDOWS):
        sl = slice(g * POOL_GROUP_DIM, (g + 1) * POOL_GROUP_DIM)
        s = cs[:, P + 1:P + 1 + T, sl] - cs[:, P + 1 - w:P + 1 - w + T, sl]
        cnt = jnp.minimum(pos + 1, w).astype(jnp.float32)
        outs.append(s / cnt[None, :, None])
    pooled = jnp.concatenate(outs, axis=-1)
    d = (pooled - u_ext[:, P:]).reshape(B, T, N_POOL_GROUPS, POOL_GROUP_DIM)
    y = jnp.einsum('btgc,gcd->btgd', d, w_group.astype(jnp.float32)).reshape(B, T, D)
    y = y * scale.astype(jnp.float32)
    new_buf = u_ext[:, T:]
    return y.astype(u.dtype), new_buf


def mlstm_chunkwise(q, k, v, i_pre, logf, C0, n0, m0):
    B, H, T, DK = q.shape
    DV = v.shape[-1]
    L = math.gcd(T, CHUNK)
    NC = T // L

    def to_chunks(a):
        return jnp.moveaxis(a.reshape(B, H, NC, L, *a.shape[3:]), 2, 0)

    xs = (to_chunks(q), to_chunks(k), to_chunks(v), to_chunks(i_pre), to_chunks(logf))
    causal = jnp.tril(jnp.ones((L, L), dtype=bool))

    def step(carry, inp):
        C, n, m = carry
        qc, kc, vc, ic, fc = inp
        b = jnp.cumsum(fc, axis=-1)
        a = b + m[..., None]
        Dm = b[..., :, None] - b[..., None, :] + ic[..., None, :]
        Dm = jnp.where(causal, Dm, -jnp.inf)
        m_row = jnp.maximum(a, jnp.max(Dm, axis=-1))
        w_intra = jnp.exp(Dm - m_row[..., None])
        w_inter = jnp.exp(a - m_row)
        s = jnp.einsum('bhik,bhjk->bhij', qc, kc) * w_intra
        num = w_inter[..., None] * jnp.einsum('bhik,bhkv->bhiv', qc, C) + jnp.einsum('bhij,bhjv->bhiv', s, vc)
        den = w_inter * jnp.einsum('bhik,bhk->bhi', qc, n) + jnp.sum(s, axis=-1)
        h = num / jnp.maximum(jnp.abs(den), jnp.exp(-m_row))[..., None]
        bL = b[..., -1]
        m_new = m_row[..., -1]
        decay = jnp.exp(bL + m - m_new)
        w_k = jnp.exp(bL[..., None] - b + ic - m_new[..., None])
        kw = kc * w_k[..., None]
        C_new = decay[..., None, None] * C + jnp.einsum('bhjk,bhjv->bhkv', kw, vc)
        n_new = decay[..., None] * n + jnp.sum(kw, axis=2)
        return (C_new, n_new, m_new), h

    carry0 = (C0.astype(jnp.float32), n0.astype(jnp.float32), m0.astype(jnp.float32))
    (C, n, m), hs = lax.scan(step, carry0, xs)
    h = jnp.moveaxis(hs, 0, 2).reshape(B, H, T, DV)
    return h, C, n, m


def mlstm_mixer(u, C0, n0, m0, w_in, b_gate, head_norm, w_out):
    B, T, _ = u.shape
    proj = u @ w_in
    splits = np.cumsum([QK_DIM, QK_DIM, V_DIM, V_DIM, MLSTM_HEADS]).tolist()
    q, k, v, o, i_g, f_g = jnp.split(proj, splits, axis=-1)
    f32 = jnp.float32
    q = q.astype(f32).reshape(B, T, MLSTM_HEADS, QK_HEAD_DIM).transpose(0, 2, 1, 3) * (QK_HEAD_DIM ** -0.5)
    k = k.astype(f32).reshape(B, T, MLSTM_HEADS, QK_HEAD_DIM).transpose(0, 2, 1, 3)
    v = v.astype(f32).reshape(B, T, MLSTM_HEADS, V_HEAD_DIM).transpose(0, 2, 1, 3)
    bg = b_gate.astype(f32)
    i_pre = GATE_CAP * jnp.tanh((i_g.astype(f32) + bg[:MLSTM_HEADS]) / GATE_CAP)
    f_pre = GATE_CAP * jnp.tanh((f_g.astype(f32) + bg[MLSTM_HEADS:]) / GATE_CAP)
    logf = jax.nn.log_sigmoid(f_pre)
    h, C, n, m = mlstm_chunkwise(q, k, v, i_pre.transpose(0, 2, 1), logf.transpose(0, 2, 1), C0, n0, m0)
    h = h.transpose(0, 2, 1, 3)
    h = h * lax.rsqrt(jnp.mean(h * h, axis=-1, keepdims=True) + EPS)
    h = h.reshape(B, T, V_DIM) * head_norm.astype(f32)
    out = (jax.nn.sigmoid(o.astype(f32)) * h).astype(u.dtype) @ w_out
    return out, C, n, m


def trunk(x, pos0, pool_state, C_state, n_state, m_state, norm_mix, norm_ffn, norm_final,
          pool_w, pool_scale, mlstm_w_in, mlstm_b_gate, mlstm_head_norm, mlstm_w_out, ffn_w_up, ffn_w_down):
    h = x
    new_pool, new_C, new_n, new_m = [], [], [], []
    for layer in range(DEPTH):
        j = layer // N_MIXERS
        u = rmsnorm(h, norm_mix[layer])
        if layer % N_MIXERS == 0:
            y, buf = pool_mixer(u, pool_state[j], pos0, pool_w[j], pool_scale[j])
            new_pool.append(buf)
        else:
            y, C, n, m = mlstm_mixer(u, C_state[j], n_state[j], m_state[j], mlstm_w_in[j],
                                     mlstm_b_gate[j], mlstm_head_norm[j], mlstm_w_out[j])
            new_C.append(C)
            new_n.append(n)
            new_m.append(m)
        h = h + y.astype(h.dtype)
        h = h + sq_relu_mlp(rmsnorm(h, norm_ffn[layer]), ffn_w_up[layer], ffn_w_down[layer]).astype(h.dtype)
    dt = x.dtype
    return (rmsnorm(h, norm_final), jnp.stack(new_pool).astype(dt), jnp.stack(new_C).astype(dt),
            jnp.stack(new_n).astype(dt), jnp.stack(new_m).astype(dt))


def setup_inputs(seed: int = 0) -> dict:
    key = jax.random.key(seed)
    ks = jax.random.split(key, 20)
    nrm = jax.random.normal
    f32 = jnp.float32
    b_i = 0.1 * nrm(ks[10], (N_MLSTM_LAYERS, MLSTM_HEADS), f32)
    b_f = jnp.linspace(3.0, 6.0, MLSTM_HEADS, dtype=f32)[None, :] + 0.1 * nrm(ks[11], (N_MLSTM_LAYERS, MLSTM_HEADS), f32)
    return {
        "x_prompt": nrm(ks[0], (BATCH, SEQ, D_MODEL), f32),
        "x_sample": nrm(ks[1], (DEC_BATCH, DEC_SEQ, D_MODEL), f32),
        "state_pool": nrm(ks[2], (N_POOL_LAYERS, DEC_BATCH, POOL_BUF, D_MODEL), f32),
        "state_mlstm_C": 0.05 * nrm(ks[3], (N_MLSTM_LAYERS, DEC_BATCH, MLSTM_HEADS, QK_HEAD_DIM, V_HEAD_DIM), f32),
        "state_mlstm_n": 0.1 * nrm(ks[4], (N_MLSTM_LAYERS, DEC_BATCH, MLSTM_HEADS, QK_HEAD_DIM), f32),
        "state_mlstm_m": nrm(ks[5], (N_MLSTM_LAYERS, DEC_BATCH, MLSTM_HEADS), f32),
        "norm_mix": 1.0 + 0.02 * nrm(ks[6], (DEPTH, D_MODEL), f32),
        "norm_ffn": 1.0 + 0.02 * nrm(ks[7], (DEPTH, D_MODEL), f32),
        "norm_final": 1.0 + 0.02 * nrm(ks[8], (D_MODEL,), f32),
        "pool_w": nrm(ks[9], (N_POOL_LAYERS, N_POOL_GROUPS, POOL_GROUP_DIM, POOL_GROUP_DIM), f32) * POOL_GROUP_DIM ** -0.5,
        "pool_scale": 1.0 + 0.02 * nrm(ks[12], (N_POOL_LAYERS, D_MODEL), f32),
        "mlstm_w_in": nrm(ks[13], (N_MLSTM_LAYERS, D_MODEL, MLSTM_IN_DIM), f32) * D_MODEL ** -0.5,
        "mlstm_b_gate": jnp.concatenate([b_i, b_f], axis=-1),
        "mlstm_head_norm": 1.0 + 0.02 * nrm(ks[14], (N_MLSTM_LAYERS, V_DIM), f32),
        "mlstm_w_out": nrm(ks[15], (N_MLSTM_LAYERS, V_DIM, D_MODEL), f32) * V_DIM ** -0.5,
        "ffn_w_up": nrm(ks[16], (DEPTH, D_MODEL, D_FF), f32) * D_MODEL ** -0.5,
        "ffn_w_down": nrm(ks[17], (DEPTH, D_FF, D_MODEL), f32) * D_FF ** -0.5,
    }


def reference(x_prompt, x_sample, state_pool, state_mlstm_C, state_mlstm_n, state_mlstm_m,
              norm_mix, norm_ffn, norm_final, pool_w, pool_scale, mlstm_w_in, mlstm_b_gate,
              mlstm_head_norm, mlstm_w_out, ffn_w_up, ffn_w_down):
    B = x_prompt.shape[0]
    dt = x_prompt.dtype
    pool0 = jnp.zeros((N_POOL_LAYERS, B, POOL_BUF, D_MODEL), dt)
    C0 = jnp.zeros((N_MLSTM_LAYERS, B, MLSTM_HEADS, QK_HEAD_DIM, V_HEAD_DIM), jnp.float32)
    n0 = jnp.zeros((N_MLSTM_LAYERS, B, MLSTM_HEADS, QK_HEAD_DIM), jnp.float32)
    m0 = jnp.zeros((N_MLSTM_LAYERS, B, MLSTM_HEADS), jnp.float32)
    y_prompt, pool_p, C_p, n_p, m_p = trunk(
        x_prompt, 0, pool0, C0, n0, m0, norm_mix, norm_ffn, norm_final, pool_w, pool_scale,
        mlstm_w_in, mlstm_b_gate, mlstm_head_norm, mlstm_w_out, ffn_w_up, ffn_w_down)
    y_sample, pool_s, C_s, n_s, m_s = trunk(
        x_sample, PAST_LEN, state_pool, state_mlstm_C, state_mlstm_n, state_mlstm_m,
        norm_mix, norm_ffn, norm_final, pool_w, pool_scale,
        mlstm_w_in, mlstm_b_gate, mlstm_head_norm, mlstm_w_out, ffn_w_up, ffn_w_down)
    return (y_prompt, y_sample, pool_p, pool_s, C_p, n_p, m_p, C_s, n_s, m_s)
```

```python
import functools

import jax
import jax.numpy as jnp
from jax import lax
from jax.experimental import pallas as pl
from jax.experimental.pallas import tpu as pltpu

F32 = jnp.float32
BF16 = jnp.bfloat16

D_MODEL = 2048
DEPTH = 4
POOL_WINDOWS = (2, 4, 8, 16)
N_GROUPS = len(POOL_WINDOWS)
GROUP_DIM = D_MODEL // N_GROUPS
POOL_BUF = max(POOL_WINDOWS) - 1
HALO = POOL_BUF + 1
HEADS = 4
QK_DIM = D_MODEL // 2
V_DIM = D_MODEL
DK = QK_DIM // HEADS
DV = V_DIM // HEADS
QKVO_DIM = 2 * QK_DIM + 2 * V_DIM
GATE_CAP = 15.0
D_FF = 4 * D_MODEL
EPS = 1e-6
LANES = 128

VMEM_LIMIT = 56 * 1024 * 1024


def _rmsnorm(x, g):
    return x * lax.rsqrt(jnp.mean(x * x, axis=-1, keepdims=True) + EPS) * g


def _params(sem):
    return pltpu.CompilerParams(dimension_semantics=sem, vmem_limit_bytes=VMEM_LIMIT)


def _pool_prompt_kernel(h_ref, g_ref, w_ref, s_ref, o_ref, nb_ref, ext_ref, *, tt):
    t = pl.program_id(1)

    @pl.when(t == 0)
    def _():
        ext_ref[0:HALO, :] = jnp.zeros((HALO, D_MODEL), F32)

    x = h_ref[...]
    ext_ref[HALO:HALO + tt, :] = _rmsnorm(x, g_ref[...])
    n_seen = t * tt + 1 + lax.broadcasted_iota(jnp.int32, (tt, 1), 0)
    outs = []
    for g, w in enumerate(POOL_WINDOWS):
        cs = slice(g * GROUP_DIM, (g + 1) * GROUP_DIM)
        cur = ext_ref[HALO:HALO + tt, cs]
        s = cur
        for i in range(1, w):
            s = s + ext_ref[HALO - i:HALO - i + tt, cs]
        d = s / jnp.minimum(n_seen, w).astype(F32) - cur
        outs.append(jnp.dot(d.astype(BF16), w_ref[g], preferred_element_type=F32))
    o_ref[...] = x + jnp.concatenate(outs, axis=-1) * s_ref[...]

    @pl.when(t == pl.num_programs(1) - 1)
    def _():
        nb_ref[0] = ext_ref[HALO + tt - POOL_BUF:HALO + tt, :]

    ext_ref[0:HALO, :] = ext_ref[tt:tt + HALO, :]


def _pool_sample_kernel(h_ref, buf_ref, g_ref, w_ref, s_ref, o_ref, nb_ref, ext_ref, *, sb, t_len):
    x = h_ref[...]
    u = _rmsnorm(x, g_ref[...])
    ext_ref[:, 1:HALO, :] = buf_ref[...]
    ext_ref[:, HALO:HALO + t_len, :] = u.reshape(sb, t_len, D_MODEL)
    outs = []
    for g, w in enumerate(POOL_WINDOWS):
        cs = slice(g * GROUP_DIM, (g + 1) * GROUP_DIM)
        cur = ext_ref[:, HALO:HALO + t_len, cs]
        s = cur
        for i in range(1, w):
            s = s + ext_ref[:, HALO - i:HALO - i + t_len, cs]
        d = (s * (1.0 / w) - cur).reshape(sb * t_len, GROUP_DIM)
        outs.append(jnp.dot(d.astype(BF16), w_ref[g], preferred_element_type=F32))
    o_ref[...] = x + jnp.concatenate(outs, axis=-1) * s_ref[...]
    nb_ref[...] = ext_ref[:, HALO + t_len - POOL_BUF:HALO + t_len, :]


def _pool_layer(h, buf_s, gain, w_bf, scale, *, n_prompt, seq, t_len):
    n_tok = h.shape[0]
    batch = n_prompt // seq
    tt = 512
    nt = seq // tt
    gain = gain.reshape(1, D_MODEL)
    scale = scale.reshape(1, D_MODEL)
    h, nb_p = pl.pallas_call(
        functools.partial(_pool_prompt_kernel, tt=tt),
        grid=(batch, nt),
        in_specs=[
            pl.BlockSpec((tt, D_MODEL), lambda b, t: (b * nt + t, 0)),
            pl.BlockSpec((1, D_MODEL), lambda b, t: (0, 0)),
            pl.BlockSpec((N_GROUPS, GROUP_DIM, GROUP_DIM), lambda b, t: (0, 0, 0)),
            pl.BlockSpec((1, D_MODEL), lambda b, t: (0, 0)),
        ],
        out_specs=[
            pl.BlockSpec((tt, D_MODEL), lambda b, t: (b * nt + t, 0)),
            pl.BlockSpec((1, POOL_BUF, D_MODEL), lambda b, t: (b, 0, 0)),
        ],
        out_shape=[
            jax.ShapeDtypeStruct((n_tok, D_MODEL), F32),
            jax.ShapeDtypeStruct((batch, POOL_BUF, D_MODEL), F32),
        ],
        scratch_shapes=[pltpu.VMEM((HALO + tt, D_MODEL), F32)],
        input_output_aliases={0: 0},
        compiler_params=_params(("arbitrary", "arbitrary")),
        name="pool_prompt",
    )(h, gain, w_bf, scale)

    dec_batch = buf_s.shape[0]
    sb = 16
    rows = sb * t_len
    first = n_prompt // rows
    h, nb_s = pl.pallas_call(
        functools.partial(_pool_sample_kernel, sb=sb, t_len=t_len),
        grid=(dec_batch // sb,),
        in_specs=[
            pl.BlockSpec((rows, D_MODEL), lambda i: (first + i, 0)),
            pl.BlockSpec((sb, POOL_BUF, D_MODEL), lambda i: (i, 0, 0)),
            pl.BlockSpec((1, D_MODEL), lambda i: (0, 0)),
            pl.BlockSpec((N_GROUPS, GROUP_DIM, GROUP_DIM), lambda i: (0, 0, 0)),
            pl.BlockSpec((1, D_MODEL), lambda i: (0, 0)),
        ],
        out_specs=[
            pl.BlockSpec((rows, D_MODEL), lambda i: (first + i, 0)),
            pl.BlockSpec((sb, POOL_BUF, D_MODEL), lambda i: (i, 0, 0)),
        ],
        out_shape=[
            jax.ShapeDtypeStruct((n_tok, D_MODEL), F32),
            jax.ShapeDtypeStruct((dec_batch, POOL_BUF, D_MODEL), F32),
        ],
        scratch_shapes=[pltpu.VMEM((sb, HALO + t_len, D_MODEL), F32)],
        input_output_aliases={0: 0},
        compiler_params=_params(("arbitrary",)),
        name="pool_sample",
    )(h, buf_s, gain, w_bf, scale)
    return h, nb_p, nb_s


def _ffn_kernel(x_ref, g_ref, wu_ref, wd_ref, gf_ref, o_ref, u_ref, *, final_norm):
    k = pl.program_id(1)

    @pl.when(k == 0)
    def _():
        x = x_ref[...]
        u_ref[...] = _rmsnorm(x, g_ref[...]).astype(BF16)
        o_ref[...] = x

    a = jnp.maximum(jnp.dot(u_ref[...], wu_ref[...], preferred_element_type=F32), 0.0)
    o_ref[...] += jnp.dot((a * a).astype(BF16), wd_ref[...], preferred_element_type=F32)

    if final_norm:
        @pl.when(k == pl.num_programs(1) - 1)
        def _():
            o_ref[...] = _rmsnorm(o_ref[...], gf_ref[...])


def _ffn_layer(h, gain, wu_bf, wd_bf, gain_final, *, final_norm):
    n_tok = h.shape[0]
    tm, fc = 512, 1024
    return pl.pallas_call(
        functools.partial(_ffn_kernel, final_norm=final_norm),
        grid=(n_tok // tm, D_FF // fc),
        in_specs=[
            pl.BlockSpec((tm, D_MODEL), lambda i, k: (i, 0)),
            pl.BlockSpec((1, D_MODEL), lambda i, k: (0, 0)),
            pl.BlockSpec((D_MODEL, fc), lambda i, k: (0, k)),
            pl.BlockSpec((fc, D_MODEL), lambda i, k: (k, 0)),
            pl.BlockSpec((1, D_MODEL), lambda i, k: (0, 0)),
        ],
        out_specs=pl.BlockSpec((tm, D_MODEL), lambda i, k: (i, 0)),
        out_shape=jax.ShapeDtypeStruct((n_tok, D_MODEL), F32),
        scratch_shapes=[pltpu.VMEM((tm, D_MODEL), BF16)],
        compiler_params=_params(("parallel", "arbitrary")),
        name="ffn",
    )(h, gain.reshape(1, D_MODEL), wu_bf, wd_bf, gain_final.reshape(1, D_MODEL))


def _mlstm_in_kernel(x_ref, g_ref, w_ref, wg_ref, bg_ref, p_ref, gc_ref, u_ref):
    j = pl.program_id(1)

    @pl.when(j == 0)
    def _():
        u = _rmsnorm(x_ref[...], g_ref[...]).astype(BF16)
        u_ref[...] = u
        pre = jnp.dot(u, wg_ref[...], preferred_element_type=F32) + bg_ref[...]
        capped = GATE_CAP * jnp.tanh(pre / GATE_CAP)
        log_f = jnp.minimum(capped, 0.0) - jnp.log1p(jnp.exp(-jnp.abs(capped)))
        lane = lax.broadcasted_iota(jnp.int32, capped.shape, 1)
        gc_ref[...] = jnp.where(lane < HEADS, capped, log_f)

    p_ref[...] = jnp.dot(u_ref[...], w_ref[...], preferred_element_type=F32).astype(BF16)


def _mlstm_in(h, gain, w_bf, wg_bf, b_gate):
    n_tok = h.shape[0]
    tm, nb = 1024, 1024
    return pl.pallas_call(
        _mlstm_in_kernel,
        grid=(n_tok // tm, QKVO_DIM // nb),
        in_specs=[
            pl.BlockSpec((tm, D_MODEL), lambda i, j: (i, 0)),
            pl.BlockSpec((1, D_MODEL), lambda i, j: (0, 0)),
            pl.BlockSpec((D_MODEL, nb), lambda i, j: (0, j)),
            pl.BlockSpec((D_MODEL, LANES), lambda i, j: (0, 0)),
            pl.BlockSpec((1, LANES), lambda i, j: (0, 0)),
        ],
        out_specs=[
            pl.BlockSpec((tm, nb), lambda i, j: (i, j)),
            pl.BlockSpec((tm, LANES), lambda i, j: (i, 0)),
        ],
        out_shape=[
            jax.ShapeDtypeStruct((n_tok, QKVO_DIM), BF16),
            jax.ShapeDtypeStruct((n_tok, LANES), F32),
        ],
        scratch_shapes=[pltpu.VMEM((tm, D_MODEL), BF16)],
        compiler_params=_params(("parallel", "arbitrary")),
        name="mlstm_in",
    )(h, gain.reshape(1, D_MODEL), w_bf, wg_bf, b_gate)


def _dot_f32(lhs, rhs, dims):
    return lax.dot_general(lhs, rhs, (dims, ((), ())), preferred_element_type=F32,
                           precision=lax.Precision.HIGHEST)


def _mlstm_rec_kernel(q_ref, k_ref, v_ref, o_ref, gc_ref, hn_ref, c0_ref, n0_ref, m0_ref,
                      y_ref, c_ref, n_ref, m_ref, *, chunk, zero_init):
    c = pl.program_id(1)

    @pl.when(c == 0)
    def _():
        if zero_init:
            c_ref[...] = jnp.zeros(c_ref.shape, F32)
            n_ref[...] = jnp.zeros(n_ref.shape, F32)
            m_ref[...] = jnp.zeros(m_ref.shape, F32)
        else:
            c_ref[...] = c0_ref[...]
            n_ref[...] = n0_ref[...]
            m_ref[...] = m0_ref[...]

    row = lax.broadcasted_iota(jnp.int32, (chunk, chunk), 0)
    col = lax.broadcasted_iota(jnp.int32, (chunk, chunk), 1)
    causal = col <= row
    gates_c = gc_ref[...]
    pick = (lax.broadcasted_iota(jnp.int32, (2 * HEADS, LANES), 0)
            == lax.broadcasted_iota(jnp.int32, (2 * HEADS, LANES), 1)).astype(F32)
    gates_r = _dot_f32(pick, gates_c, ((1,), (1,)))
    cum_c = _dot_f32(causal.astype(F32), gates_c, ((1,), (0,)))
    cum_r = _dot_f32(gates_r, (row <= col).astype(F32), ((1,), (0,)))

    for hd in range(HEADS):
        q = q_ref[:, hd * DK:(hd + 1) * DK] * BF16(DK ** -0.5)
        k = k_ref[:, hd * DK:(hd + 1) * DK]
        v = v_ref[:, hd * DV:(hd + 1) * DV]
        i_col = gates_c[:, hd:hd + 1]
        i_row = gates_r[hd:hd + 1, :]
        b_col = cum_c[:, HEADS + hd:HEADS + hd + 1]
        b_row = cum_r[HEADS + hd:HEADS + hd + 1, :]
        c_prev = c_ref[0, hd]
        n_prev = n_ref[0, hd:hd + 1, :]
        m_prev = m_ref[0, :, hd:hd + 1]

        a = b_col + m_prev
        dm = jnp.where(causal, b_col - b_row + i_row, -jnp.inf)
        m_row = jnp.maximum(a, jnp.max(dm, axis=1, keepdims=True))
        w_intra = jnp.exp(dm - m_row)
        w_inter = jnp.exp(a - m_row)
        s = lax.dot_general(q, k, (((1,), (1,)), ((), ())), preferred_element_type=F32) * w_intra
        num = (w_inter * jnp.dot(q, c_prev.astype(BF16), preferred_element_type=F32)
               + jnp.dot(s.astype(BF16), v, preferred_element_type=F32))
        den = (w_inter * jnp.sum(q.astype(F32) * n_prev, axis=1, keepdims=True)
               + jnp.sum(s, axis=1, keepdims=True))
        hh = num / jnp.maximum(jnp.abs(den), jnp.exp(-m_row))

        b_last = b_col[chunk - 1:chunk, :]
        m_new = m_row[chunk - 1:chunk, :]
        decay = jnp.exp(b_last + m_prev - m_new)
        kw = k.astype(F32) * jnp.exp(b_last - b_col + i_col - m_new)
        c_ref[0, hd] = decay * c_prev + lax.dot_general(
            kw.astype(BF16), v, (((0,), (0,)), ((), ())), preferred_element_type=F32)
        n_ref[0, hd:hd + 1, :] = decay * n_prev + jnp.sum(kw, axis=0, keepdims=True)
        m_ref[0, :, hd:hd + 1] = m_new

        hh = hh * lax.rsqrt(jnp.mean(hh * hh, axis=-1, keepdims=True) + EPS)
        hh = hh * hn_ref[:, hd * DV:(hd + 1) * DV]
        o = o_ref[:, hd * DV:(hd + 1) * DV].astype(F32)
        y_ref[:, hd * DV:(hd + 1) * DV] = (jax.nn.sigmoid(o) * hh).astype(BF16)


def _mlstm_rec(proj, gates_c, head_norm, state, *, first_row, n_seq, t_len, chunk):
    n_chunks = t_len // chunk
    base = first_row // chunk
    zero_init = state is None
    if zero_init:
        state = (jnp.zeros((1, HEADS, DK, DV), F32), jnp.zeros((1, HEADS, DK), F32),
                 jnp.zeros((1, 1, LANES), F32))
        smap4 = lambda b, c: (0, 0, 0, 0)
        smap3 = lambda b, c: (0, 0, 0)
    else:
        smap4 = lambda b, c: (b, 0, 0, 0)
        smap3 = lambda b, c: (b, 0, 0)
    rmap = lambda blk: (lambda b, c: (base + b * n_chunks + c, blk))
    return pl.pallas_call(
        functools.partial(_mlstm_rec_kernel, chunk=chunk, zero_init=zero_init),
        grid=(n_seq, n_chunks),
        in_specs=[
            pl.BlockSpec((chunk, QK_DIM), rmap(0)),
            pl.BlockSpec((chunk, QK_DIM), rmap(1)),
            pl.BlockSpec((chunk, V_DIM), rmap(1)),
            pl.BlockSpec((chunk, V_DIM), rmap(2)),
            pl.BlockSpec((chunk, LANES), rmap(0)),
            pl.BlockSpec((1, V_DIM), lambda b, c: (0, 0)),
            pl.BlockSpec((1, HEADS, DK, DV), smap4),
            pl.BlockSpec((1, HEADS, DK), smap3),
            pl.BlockSpec((1, 1, LANES), smap3),
        ],
        out_specs=[
            pl.BlockSpec((chunk, V_DIM), lambda b, c: (b * n_chunks + c, 0)),
            pl.BlockSpec((1, HEADS, DK, DV), lambda b, c: (b, 0, 0, 0)),
            pl.BlockSpec((1, HEADS, DK), lambda b, c: (b, 0, 0)),
            pl.BlockSpec((1, 1, LANES), lambda b, c: (b, 0, 0)),
        ],
        out_shape=[
            jax.ShapeDtypeStruct((n_seq * t_len, V_DIM), BF16),
            jax.ShapeDtypeStruct((n_seq, HEADS, DK, DV), F32),
            jax.ShapeDtypeStruct((n_seq, HEADS, DK), F32),
            jax.ShapeDtypeStruct((n_seq, 1, LANES), F32),
        ],
        compiler_params=_params(("arbitrary", "arbitrary")),
        name="mlstm_rec_prompt" if zero_init else "mlstm_rec_sample",
    )(proj, proj, proj, proj, gates_c, head_norm.reshape(1, V_DIM), *state)


def _out_proj_kernel(x_ref, y_ref, w_ref, o_ref):
    o_ref[...] = x_ref[...] + jnp.dot(y_ref[...], w_ref[...], preferred_element_type=F32)


def _out_proj(h, y, w_bf):
    n_tok = h.shape[0]
    tm = 512
    return pl.pallas_call(
        _out_proj_kernel,
        grid=(n_tok // tm,),
        in_specs=[
            pl.BlockSpec((tm, D_MODEL), lambda i: (i, 0)),
            pl.BlockSpec((tm, V_DIM), lambda i: (i, 0)),
            pl.BlockSpec((V_DIM, D_MODEL), lambda i: (0, 0)),
        ],
        out_specs=pl.BlockSpec((tm, D_MODEL), lambda i: (i, 0)),
        out_shape=jax.ShapeDtypeStruct((n_tok, D_MODEL), F32),
        compiler_params=_params(("parallel",)),
        name="mlstm_out",
    )(h, y, w_bf)


def _mlstm_layer(h, state_s, gain, w_in, b_gate, head_norm, w_out, *, n_prompt, seq, t_len):
    w_bf = w_in[:, :QKVO_DIM].astype(BF16)
    wg_bf = jnp.pad(w_in[:, QKVO_DIM:], ((0, 0), (0, LANES - 2 * HEADS))).astype(BF16)
    bg = jnp.pad(b_gate, (0, LANES - 2 * HEADS)).reshape(1, LANES)
    proj, gates_c = _mlstm_in(h, gain, w_bf, wg_bf, bg)

    batch = n_prompt // seq
    y_p, c_p, n_p, m_p = _mlstm_rec(proj, gates_c, head_norm, None,
                                    first_row=0, n_seq=batch, t_len=seq, chunk=256)
    c_s, n_s, m_s = state_s
    dec_batch = c_s.shape[0]
    m_s = jnp.pad(m_s, ((0, 0), (0, LANES - HEADS))).reshape(dec_batch, 1, LANES)
    y_s, c_s, n_s, m_s = _mlstm_rec(proj, gates_c, head_norm, (c_s, n_s, m_s),
                                    first_row=n_prompt, n_seq=dec_batch, t_len=t_len, chunk=t_len)
    h = _out_proj(h, jnp.concatenate([y_p, y_s], axis=0), w_out.astype(BF16))
    return h, (c_p, n_p, m_p[:, 0, :HEADS]), (c_s, n_s, m_s[:, 0, :HEADS])


def kernel(x_prompt, x_sample, state_pool, state_mlstm_C, state_mlstm_n, state_mlstm_m, norm_mix, norm_ffn, norm_final, pool_w, pool_scale, mlstm_w_in, mlstm_b_gate, mlstm_head_norm, mlstm_w_out, ffn_w_up, ffn_w_down):
    batch, seq, _ = x_prompt.shape
    dec_batch, t_len, _ = x_sample.shape
    n_prompt = batch * seq
    h = jnp.concatenate([x_prompt.reshape(n_prompt, D_MODEL),
                         x_sample.reshape(dec_batch * t_len, D_MODEL)], axis=0)
    pool_p, pool_s, st_p, st_s = [], [], [], []
    for layer in range(DEPTH):
        j = layer // 2
        if layer % 2 == 0:
            h, nb_p, nb_s = _pool_layer(h, state_pool[j], norm_mix[layer], pool_w[j].astype(BF16),
                                        pool_scale[j], n_prompt=n_prompt, seq=seq, t_len=t_len)
            pool_p.append(nb_p)
            pool_s.append(nb_s)
        else:
            h, sp, ss = _mlstm_layer(h, (state_mlstm_C[j], state_mlstm_n[j], state_mlstm_m[j]),
                                     norm_mix[layer], mlstm_w_in[j], mlstm_b_gate[j], mlstm_head_norm[j],
                                     mlstm_w_out[j], n_prompt=n_prompt, seq=seq, t_len=t_len)
            st_p.append(sp)
            st_s.append(ss)
        h = _ffn_layer(h, norm_ffn[layer], ffn_w_up[layer].astype(BF16), ffn_w_down[layer].astype(BF16),
                       norm_final, final_norm=(layer == DEPTH - 1))
    y_prompt = h[:n_prompt].reshape(batch, seq, D_MODEL)
    y_sample = h[n_prompt:].reshape(dec_batch, t_len, D_MODEL)
    stack = lambda xs, i: jnp.stack([x[i] for x in xs])
    return (y_prompt, y_sample, jnp.stack(pool_p), jnp.stack(pool_s),
            stack(st_p, 0), stack(st_p, 1), stack(st_p, 2),
            stack(st_s, 0), stack(st_s, 1), stack(st_s, 2))
```

```python
import functools

import jax
import jax.numpy as jnp
from jax import lax
from jax.experimental import pallas as pl
from jax.experimental.pallas import tpu as pltpu

F32 = jnp.float32
BF16 = jnp.bfloat16

D_MODEL = 2048
DEPTH = 4
N_MIX = DEPTH // 2
POOL_WINDOWS = (2, 4, 8, 16)
N_GROUPS = len(POOL_WINDOWS)
GROUP_DIM = D_MODEL // N_GROUPS
POOL_BUF = max(POOL_WINDOWS) - 1
HALO = POOL_BUF + 1
HEADS = 4
QK_DIM = D_MODEL // 2
V_DIM = D_MODEL
DK = QK_DIM // HEADS
DV = V_DIM // HEADS
QKVO_DIM = 2 * QK_DIM + 2 * V_DIM
GATE_CAP = 15.0
D_FF = 4 * D_MODEL
EPS = 1e-6
LANES = 128

VMEM_LIMIT = 56 * 1024 * 1024


def _rmsnorm(x, g):
    return x * lax.rsqrt(jnp.mean(x * x, axis=-1, keepdims=True) + EPS) * g


def _call(body, *, name, grid, sem, in_specs, out_specs, out_shape, operands, carried=None, aliases=None,
          scratch_shapes=()):
    carried = carried or {}
    aliases = dict(aliases or {})
    for pos, out_idx in enumerate(carried):
        aliases[len(operands) + pos] = out_idx
    return pl.pallas_call(
        body,
        grid=grid,
        in_specs=list(in_specs) + [pl.BlockSpec(memory_space=pl.ANY)] * len(carried),
        out_specs=out_specs,
        out_shape=out_shape,
        scratch_shapes=list(scratch_shapes),
        input_output_aliases=aliases,
        compiler_params=pltpu.CompilerParams(dimension_semantics=sem, vmem_limit_bytes=VMEM_LIMIT),
        name=name,
    )(*operands, *carried.values())


def _layer_row(layer, d=D_MODEL):
    return pl.BlockSpec((None, 1, d), lambda *_: (layer, 0, 0))


def _pool_prompt_kernel(h_ref, g_ref, w_ref, s_ref, *rest, tt, n_carried):
    o_ref, nb_ref, ext_ref = rest[n_carried:]
    t = pl.program_id(1)

    @pl.when(t == 0)
    def _():
        ext_ref[0:HALO, :] = jnp.zeros((HALO, D_MODEL), F32)

    x = h_ref[...]
    ext_ref[HALO:HALO + tt, :] = _rmsnorm(x, g_ref[...])
    n_seen = t * tt + 1 + lax.broadcasted_iota(jnp.int32, (tt, 1), 0)
    outs = []
    for g, w in enumerate(POOL_WINDOWS):
        cs = slice(g * GROUP_DIM, (g + 1) * GROUP_DIM)
        cur = ext_ref[HALO:HALO + tt, cs]
        s = cur
        for i in range(1, w):
            s = s + ext_ref[HALO - i:HALO - i + tt, cs]
        d = s / jnp.minimum(n_seen, w).astype(F32) - cur
        outs.append(jnp.dot(d.astype(BF16), w_ref[g], preferred_element_type=F32))
    o_ref[...] = x + jnp.concatenate(outs, axis=-1) * s_ref[...]

    @pl.when(t == pl.num_programs(1) - 1)
    def _():
        nb_ref[0] = ext_ref[HALO + tt - POOL_BUF:HALO + tt, :]

    ext_ref[0:HALO, :] = ext_ref[tt:tt + HALO, :]


def _pool_sample_kernel(h_ref, buf_ref, g_ref, w_ref, s_ref, *rest, sb, t_len, n_carried):
    o_ref, nb_ref, ext_ref = rest[n_carried:]
    x = h_ref[...]
    u = _rmsnorm(x, g_ref[...])
    ext_ref[:, 1:HALO, :] = buf_ref[...]
    ext_ref[:, HALO:HALO + t_len, :] = u.reshape(sb, t_len, D_MODEL)
    outs = []
    for g, w in enumerate(POOL_WINDOWS):
        cs = slice(g * GROUP_DIM, (g + 1) * GROUP_DIM)
        cur = ext_ref[:, HALO:HALO + t_len, cs]
        s = cur
        for i in range(1, w):
            s = s + ext_ref[:, HALO - i:HALO - i + t_len, cs]
        d = (s * (1.0 / w) - cur).reshape(sb * t_len, GROUP_DIM)
        outs.append(jnp.dot(d.astype(BF16), w_ref[g], preferred_element_type=F32))
    o_ref[...] = x + jnp.concatenate(outs, axis=-1) * s_ref[...]
    nb_ref[...] = ext_ref[:, HALO + t_len - POOL_BUF:HALO + t_len, :]


def _pool_layer(j, layer, h, x_prompt, x_sample, state_pool, new_pool, gains, w_bf, scales, dims):
    batch, seq, dec_batch, t_len = dims
    n_prompt, n_tok = batch * seq, batch * seq + dec_batch * t_len
    first = h is None
    h_shape = jax.ShapeDtypeStruct((n_tok, D_MODEL), F32)
    w_spec = pl.BlockSpec((None, N_GROUPS, GROUP_DIM, GROUP_DIM), lambda *_: (j, 0, 0, 0))

    tt = 512
    nt = seq // tt
    row_spec = pl.BlockSpec((tt, D_MODEL), lambda b, t: (b * nt + t, 0))
    h, nb_p = _call(
        functools.partial(_pool_prompt_kernel, tt=tt, n_carried=0 if first else 1),
        name="pool_prompt", grid=(batch, nt), sem=("arbitrary", "arbitrary"),
        in_specs=[row_spec, _layer_row(layer), w_spec, _layer_row(j)],
        out_specs=[row_spec, pl.BlockSpec((None, 1, POOL_BUF, D_MODEL), lambda b, t: (j, b, 0, 0))],
        out_shape=[h_shape, jax.ShapeDtypeStruct((N_MIX, batch, POOL_BUF, D_MODEL), F32)],
        operands=(x_prompt if first else h, gains, w_bf, scales),
        carried=None if first else {1: new_pool[0]},
        aliases=None if first else {0: 0},
        scratch_shapes=[pltpu.VMEM((HALO + tt, D_MODEL), F32)])

    sb = 16
    rows = sb * t_len
    base = n_prompt // rows
    out_rows = pl.BlockSpec((rows, D_MODEL), lambda i: (base + i, 0))
    h, nb_s = _call(
        functools.partial(_pool_sample_kernel, sb=sb, t_len=t_len, n_carried=1),
        name="pool_sample", grid=(dec_batch // sb,), sem=("arbitrary",),
        in_specs=[pl.BlockSpec((rows, D_MODEL), lambda i: (i, 0)) if first else out_rows,
                  pl.BlockSpec((None, sb, POOL_BUF, D_MODEL), lambda i: (j, i, 0, 0)),
                  _layer_row(layer), w_spec, _layer_row(j)],
        out_specs=[out_rows, pl.BlockSpec((None, sb, POOL_BUF, D_MODEL), lambda i: (j, i, 0, 0))],
        out_shape=[h_shape, jax.ShapeDtypeStruct((N_MIX, dec_batch, POOL_BUF, D_MODEL), F32)],
        operands=(x_sample if first else h, state_pool, gains, w_bf, scales),
        carried={0: h} if first else {1: new_pool[1]},
        aliases=None if first else {0: 0},
        scratch_shapes=[pltpu.VMEM((sb, HALO + t_len, D_MODEL), F32)])
    return h, (nb_p, nb_s)


def _ffn_tile(x_ref, g_ref, wu_ref, wd_ref, o_ref, u_ref):
    @pl.when(pl.program_id(1) == 0)
    def _():
        x = x_ref[...]
        u_ref[...] = _rmsnorm(x, g_ref[...]).astype(BF16)
        o_ref[...] = x

    a = jnp.maximum(jnp.dot(u_ref[...], wu_ref[...], preferred_element_type=F32), 0.0)
    o_ref[...] += jnp.dot((a * a).astype(BF16), wd_ref[...], preferred_element_type=F32)


def _ffn_kernel(x_ref, g_ref, wu_ref, wd_ref, o_ref, u_ref):
    _ffn_tile(x_ref, g_ref, wu_ref, wd_ref, o_ref, u_ref)


def _ffn_final_kernel(x_ref, g_ref, wu_ref, wd_ref, gf_ref, yp_ref, ys_ref, u_ref, *, n_prompt_tiles):
    def tile(o_ref):
        _ffn_tile(x_ref, g_ref, wu_ref, wd_ref, o_ref, u_ref)

        @pl.when(pl.program_id(1) == pl.num_programs(1) - 1)
        def _():
            o_ref[...] = _rmsnorm(o_ref[...], gf_ref[...])

    is_prompt = pl.program_id(0) < n_prompt_tiles
    pl.when(is_prompt)(lambda: tile(yp_ref))
    pl.when(jnp.logical_not(is_prompt))(lambda: tile(ys_ref))


def _ffn_layer(layer, h, gains, wu_bf, wd_bf, gain_final, n_prompt):
    n_tok = h.shape[0]
    tm, fc = 512, 1024
    row_spec = pl.BlockSpec((tm, D_MODEL), lambda i, k: (i, 0))
    in_specs = [row_spec, _layer_row(layer),
                pl.BlockSpec((None, D_MODEL, fc), lambda i, k: (layer, 0, k)),
                pl.BlockSpec((None, fc, D_MODEL), lambda i, k: (layer, k, 0))]
    scratch = [pltpu.VMEM((tm, D_MODEL), BF16)]
    grid = (n_tok // tm, D_FF // fc)
    if layer < DEPTH - 1:
        return _call(_ffn_kernel, name="ffn", grid=grid, sem=("parallel", "arbitrary"),
                     in_specs=in_specs, out_specs=row_spec,
                     out_shape=jax.ShapeDtypeStruct((n_tok, D_MODEL), F32),
                     operands=(h, gains, wu_bf, wd_bf), scratch_shapes=scratch)
    npt = n_prompt // tm
    return _call(
        functools.partial(_ffn_final_kernel, n_prompt_tiles=npt),
        name="ffn_final", grid=grid, sem=("arbitrary", "arbitrary"),
        in_specs=in_specs + [pl.BlockSpec((1, D_MODEL), lambda i, k: (0, 0))],
        out_specs=[pl.BlockSpec((tm, D_MODEL), lambda i, k: (jnp.minimum(i, npt - 1), 0)),
                   pl.BlockSpec((tm, D_MODEL), lambda i, k: (jnp.maximum(i - npt, 0), 0))],
        out_shape=[jax.ShapeDtypeStruct((n_prompt, D_MODEL), F32),
                   jax.ShapeDtypeStruct((n_tok - n_prompt, D_MODEL), F32)],
        operands=(h, gains, wu_bf, wd_bf, gain_final), scratch_shapes=scratch)


def _mlstm_in_kernel(x_ref, g_ref, w_ref, wg_ref, bg_ref, p_ref, gc_ref, u_ref):
    jj = pl.program_id(1)

    @pl.when(jj == 0)
    def _():
        u = _rmsnorm(x_ref[...], g_ref[...]).astype(BF16)
        u_ref[...] = u
        pre = jnp.dot(u, wg_ref[...], preferred_element_type=F32) + bg_ref[...]
        capped = GATE_CAP * jnp.tanh(pre / GATE_CAP)
        log_f = jnp.minimum(capped, 0.0) - jnp.log1p(jnp.exp(-jnp.abs(capped)))
        lane = lax.broadcasted_iota(jnp.int32, capped.shape, 1)
        gc_ref[...] = jnp.where(lane < HEADS, capped, log_f)

    p_ref[...] = jnp.dot(u_ref[...], w_ref[...], preferred_element_type=F32).astype(BF16)


def _mlstm_in(j, layer, h, gains, w_bf, wg_bf, b_gate):
    n_tok = h.shape[0]
    tm, nb = 1024, 1024
    return _call(
        _mlstm_in_kernel, name="mlstm_in", grid=(n_tok // tm, QKVO_DIM // nb), sem=("parallel", "arbitrary"),
        in_specs=[pl.BlockSpec((tm, D_MODEL), lambda i, jj: (i, 0)), _layer_row(layer),
                  pl.BlockSpec((None, D_MODEL, nb), lambda i, jj: (j, 0, jj)),
                  pl.BlockSpec((None, D_MODEL, LANES), lambda i, jj: (j, 0, 0)),
                  _layer_row(j, LANES)],
        out_specs=[pl.BlockSpec((tm, nb), lambda i, jj: (i, jj)),
                   pl.BlockSpec((tm, LANES), lambda i, jj: (i, 0))],
        out_shape=[jax.ShapeDtypeStruct((n_tok, QKVO_DIM), BF16),
                   jax.ShapeDtypeStruct((n_tok, LANES), F32)],
        operands=(h, gains, w_bf, wg_bf, b_gate),
        scratch_shapes=[pltpu.VMEM((tm, D_MODEL), BF16)])


def _dot_f32(lhs, rhs, dims):
    return lax.dot_general(lhs, rhs, (dims, ((), ())), preferred_element_type=F32,
                           precision=lax.Precision.HIGHEST)


def _mlstm_rec_kernel(q_ref, k_ref, v_ref, o_ref, gc_ref, hn_ref, *rest, chunk, n_sub, zero_init, n_carried):
    if zero_init:
        c0_ref = n0_ref = m0_ref = None
    else:
        c0_ref, n0_ref, m0_ref = rest[:3]
        rest = rest[3:]
    y_ref, c_ref, n_ref, m_ref = rest[n_carried:]

    @pl.when(pl.program_id(1) == 0)
    def _():
        if zero_init:
            c_ref[...] = jnp.zeros(c_ref.shape, F32)
            n_ref[...] = jnp.zeros(n_ref.shape, F32)
            m_ref[...] = jnp.zeros(m_ref.shape, F32)
        else:
            c_ref[...] = c0_ref[...]
            n_ref[...] = n0_ref[...]
            m_ref[...] = m0_ref[...]

    row = lax.broadcasted_iota(jnp.int32, (chunk, chunk), 0)
    col = lax.broadcasted_iota(jnp.int32, (chunk, chunk), 1)
    causal = col <= row
    pick = (lax.broadcasted_iota(jnp.int32, (2 * HEADS, LANES), 0)
            == lax.broadcasted_iota(jnp.int32, (2 * HEADS, LANES), 1)).astype(F32)

    for sub in range(n_sub):
        rows = slice(sub * chunk, (sub + 1) * chunk)
        gates_c = gc_ref[rows, :]
        gates_r = _dot_f32(pick, gates_c, ((1,), (1,)))
        cum_c = _dot_f32(causal.astype(F32), gates_c, ((1,), (0,)))
        cum_r = _dot_f32(gates_r, (row <= col).astype(F32), ((1,), (0,)))

        for hd in range(HEADS):
            q = q_ref[rows, hd * DK:(hd + 1) * DK] * BF16(DK ** -0.5)
            k = k_ref[rows, hd * DK:(hd + 1) * DK]
            v = v_ref[rows, hd * DV:(hd + 1) * DV]
            i_col = gates_c[:, hd:hd + 1]
            i_row = gates_r[hd:hd + 1, :]
            b_col = cum_c[:, HEADS + hd:HEADS + hd + 1]
            b_row = cum_r[HEADS + hd:HEADS + hd + 1, :]
            c_prev = c_ref[sub, hd]
            n_prev = n_ref[sub, hd:hd + 1, :]
            m_prev = m_ref[sub, :, hd:hd + 1]

            a = b_col + m_prev
            dm = jnp.where(causal, b_col - b_row + i_row, -jnp.inf)
            m_row = jnp.maximum(a, jnp.max(dm, axis=1, keepdims=True))
            w_intra = jnp.exp(dm - m_row)
            w_inter = jnp.exp(a - m_row)
            s = lax.dot_general(q, k, (((1,), (1,)), ((), ())), preferred_element_type=F32) * w_intra
            num = (w_inter * jnp.dot(q, c_prev.astype(BF16), preferred_element_type=F32)
                   + jnp.dot(s.astype(BF16), v, preferred_element_type=F32))
            den = (w_inter * jnp.sum(q.astype(F32) * n_prev, axis=1, keepdims=True)
                   + jnp.sum(s, axis=1, keepdims=True))
            hh = num / jnp.maximum(jnp.abs(den), jnp.exp(-m_row))

            b_last = b_col[chunk - 1:chunk, :]
            m_new = m_row[chunk - 1:chunk, :]
            decay = jnp.exp(b_last + m_prev - m_new)
            kw = k.astype(F32) * jnp.exp(b_last - b_col + i_col - m_new)
            c_ref[sub, hd] = decay * c_prev + lax.dot_general(
                kw.astype(BF16), v, (((0,), (0,)), ((), ())), preferred_element_type=F32)
            n_ref[sub, hd:hd + 1, :] = decay * n_prev + jnp.sum(kw, axis=0, keepdims=True)
            m_ref[sub, :, hd:hd + 1] = m_new

            hh = hh * lax.rsqrt(jnp.mean(hh * hh, axis=-1, keepdims=True) + EPS)
            hh = hh * hn_ref[:, hd * DV:(hd + 1) * DV]
            o = o_ref[rows, hd * DV:(hd + 1) * DV].astype(F32)
            y_ref[rows, hd * DV:(hd + 1) * DV] = (jax.nn.sigmoid(o) * hh).astype(BF16)


def _mlstm_rec(j, proj, gates_c, head_norm, state, y, new_state, *, first_row, n_seq, t_len, chunk, n_sub):
    n_tok = proj.shape[0]
    n_chunks = t_len // chunk
    blk = n_sub * chunk
    base = first_row // blk
    zero_init = state is None
    rows = lambda width_blk: (lambda b, c: (base + b * n_chunks + c, width_blk))
    st_specs = [pl.BlockSpec((None, n_sub, HEADS, DK, DV), lambda b, c: (j, b, 0, 0, 0)),
                pl.BlockSpec((None, n_sub, HEADS, DK), lambda b, c: (j, b, 0, 0)),
                pl.BlockSpec((None, n_sub, 1, LANES), lambda b, c: (j, b, 0, 0))]
    carried = {}
    if y is not None:
        carried[0] = y
    if new_state is not None:
        carried.update({1: new_state[0], 2: new_state[1], 3: new_state[2]})
    outs = _call(
        functools.partial(_mlstm_rec_kernel, chunk=chunk, n_sub=n_sub, zero_init=zero_init,
                          n_carried=len(carried)),
        name="mlstm_rec_prompt" if zero_init else "mlstm_rec_sample",
        grid=(n_seq // n_sub, n_chunks), sem=("arbitrary", "arbitrary"),
        in_specs=[pl.BlockSpec((blk, QK_DIM), rows(0)),
                  pl.BlockSpec((blk, QK_DIM), rows(1)),
                  pl.BlockSpec((blk, V_DIM), rows(1)),
                  pl.BlockSpec((blk, V_DIM), rows(2)),
                  pl.BlockSpec((blk, LANES), rows(0)),
                  _layer_row(j)] + ([] if zero_init else st_specs),
        out_specs=[pl.BlockSpec((blk, V_DIM), rows(0))] + st_specs,
        out_shape=[jax.ShapeDtypeStruct((n_tok, V_DIM), BF16),
                   jax.ShapeDtypeStruct((N_MIX, n_seq, HEADS, DK, DV), F32),
                   jax.ShapeDtypeStruct((N_MIX, n_seq, HEADS, DK), F32),
                   jax.ShapeDtypeStruct((N_MIX, n_seq, 1, LANES), F32)],
        operands=(proj, proj, proj, proj, gates_c, head_norm) + (() if zero_init else tuple(state)),
        carried=carried)
    return outs[0], tuple(outs[1:])


def _out_proj_kernel(x_ref, y_ref, w_ref, o_ref):
    o_ref[...] = x_ref[...] + jnp.dot(y_ref[...], w_ref[...], preferred_element_type=F32)


def _out_proj(j, h, y, w_bf):
    n_tok = h.shape[0]
    tm = 512
    row_spec = pl.BlockSpec((tm, D_MODEL), lambda i: (i, 0))
    return _call(
        _out_proj_kernel, name="mlstm_out", grid=(n_tok // tm,), sem=("parallel",),
        in_specs=[row_spec, pl.BlockSpec((tm, V_DIM), lambda i: (i, 0)),
                  pl.BlockSpec((None, V_DIM, D_MODEL), lambda i: (j, 0, 0))],
        out_specs=row_spec, out_shape=jax.ShapeDtypeStruct((n_tok, D_MODEL), F32),
        operands=(h, y, w_bf))


def _mlstm_layer(j, layer, h, state_s, new_p, new_s, gains, w_bf, wg_bf, b_gate, head_norm, w_out_bf, dims):
    batch, seq, dec_batch, t_len = dims
    proj, gates_c = _mlstm_in(j, layer, h, gains, w_bf, wg_bf, b_gate)
    y, new_p = _mlstm_rec(j, proj, gates_c, head_norm, None, None, new_p,
                          first_row=0, n_seq=batch, t_len=seq, chunk=256, n_sub=1)
    y, new_s = _mlstm_rec(j, proj, gates_c, head_norm, state_s, y, new_s,
                          first_row=batch * seq, n_seq=dec_batch, t_len=t_len, chunk=t_len, n_sub=4)
    return _out_proj(j, h, y, w_out_bf), new_p, new_s


def kernel(x_prompt, x_sample, state_pool, state_mlstm_C, state_mlstm_n, state_mlstm_m, norm_mix, norm_ffn, norm_final, pool_w, pool_scale, mlstm_w_in, mlstm_b_gate, mlstm_head_norm, mlstm_w_out, ffn_w_up, ffn_w_down):
    batch, seq, _ = x_prompt.shape
    dec_batch, t_len, _ = x_sample.shape
    dims = (batch, seq, dec_batch, t_len)
    n_prompt = batch * seq
    x_prompt = x_prompt.reshape(n_prompt, D_MODEL)
    x_sample = x_sample.reshape(dec_batch * t_len, D_MODEL)

    rows3 = lambda a: a.reshape(a.shape[0], 1, a.shape[1])
    norm_mix, norm_ffn, pool_scale, head_norm = map(rows3, (norm_mix, norm_ffn, pool_scale, mlstm_head_norm))
    pad_lanes = lambda a: jnp.pad(a, [(0, 0)] * (a.ndim - 1) + [(0, LANES - a.shape[-1])])
    b_gate = rows3(pad_lanes(mlstm_b_gate))
    state_s = (state_mlstm_C, state_mlstm_n, pad_lanes(state_mlstm_m)[:, :, None, :])
    pool_w_bf = pool_w.astype(BF16)
    w_in_bf = mlstm_w_in[:, :, :QKVO_DIM].astype(BF16)
    w_gate_bf = pad_lanes(mlstm_w_in[:, :, QKVO_DIM:]).astype(BF16)
    w_out_bf = mlstm_w_out.astype(BF16)
    w_up_bf = ffn_w_up.astype(BF16)
    w_down_bf = ffn_w_down.astype(BF16)
    norm_final = norm_final.reshape(1, D_MODEL)

    h = new_pool = new_p = new_s = None
    for layer in range(DEPTH):
        j = layer // 2
        if layer % 2 == 0:
            h, new_pool = _pool_layer(j, layer, h, x_prompt, x_sample, state_pool, new_pool,
                                      norm_mix, pool_w_bf, pool_scale, dims)
        else:
            h, new_p, new_s = _mlstm_layer(j, layer, h, state_s, new_p, new_s, norm_mix, w_in_bf, w_gate_bf,
                                           b_gate, head_norm, w_out_bf, dims)
        h = _ffn_layer(layer, h, norm_ffn, w_up_bf, w_down_bf, norm_final, n_prompt)

    y_prompt = h[0].reshape(batch, seq, D_MODEL)
    y_sample = h[1].reshape(dec_batch, t_len, D_MODEL)
    return (y_prompt, y_sample, new_pool[0], new_pool[1],
            new_p[0], new_p[1], new_p[2][:, :, 0, :HEADS],
            new_s[0], new_s[1], new_s[2][:, :, 0, :HEADS])
```

```python
import functools

import jax
import jax.numpy as jnp
from jax import lax
from jax.experimental import pallas as pl
from jax.experimental.pallas import tpu as pltpu

F32 = jnp.float32
BF16 = jnp.bfloat16

D_MODEL = 2048
DEPTH = 4
N_MIX = DEPTH // 2
POOL_WINDOWS = (2, 4, 8, 16)
N_GROUPS = len(POOL_WINDOWS)
GROUP_DIM = D_MODEL // N_GROUPS
POOL_BUF = max(POOL_WINDOWS) - 1
HALO = POOL_BUF + 1
HEADS = 4
QK_DIM = D_MODEL // 2
V_DIM = D_MODEL
DK = QK_DIM // HEADS
DV = V_DIM // HEADS
QKVO_DIM = 2 * QK_DIM + 2 * V_DIM
GATE_CAP = 15.0
D_FF = 4 * D_MODEL
EPS = 1e-6
LANES = 128

VMEM_LIMIT = 56 * 1024 * 1024


def _rmsnorm(x, g):
    return x * lax.rsqrt(jnp.mean(x * x, axis=-1, keepdims=True) + EPS) * g


def _call(body, *, name, grid, sem, in_specs, out_specs, out_shape, operands, carried=None, aliases=None,
          scratch_shapes=()):
    carried = carried or {}
    aliases = dict(aliases or {})
    for pos, out_idx in enumerate(carried):
        aliases[len(operands) + pos] = out_idx
    return pl.pallas_call(
        body,
        grid=grid,
        in_specs=list(in_specs) + [pl.BlockSpec(memory_space=pl.ANY)] * len(carried),
        out_specs=out_specs,
        out_shape=out_shape,
        scratch_shapes=list(scratch_shapes),
        input_output_aliases=aliases,
        compiler_params=pltpu.CompilerParams(dimension_semantics=sem, vmem_limit_bytes=VMEM_LIMIT),
        name=name,
    )(*operands, *carried.values())


def _layer_row(layer, d=D_MODEL):
    return pl.BlockSpec((None, 1, d), lambda *_: (layer, 0, 0))


def _pool_prompt_kernel(h_ref, g_ref, w_ref, s_ref, *rest, tt, n_carried):
    o_ref, nb_ref, ext_ref = rest[n_carried:]
    t = pl.program_id(1)

    @pl.when(t == 0)
    def _():
        ext_ref[0:HALO, :] = jnp.zeros((HALO, D_MODEL), F32)

    x = h_ref[...]
    ext_ref[HALO:HALO + tt, :] = _rmsnorm(x, g_ref[...])
    n_seen = t * tt + 1 + lax.broadcasted_iota(jnp.int32, (tt, 1), 0)
    outs = []
    for g, w in enumerate(POOL_WINDOWS):
        cs = slice(g * GROUP_DIM, (g + 1) * GROUP_DIM)
        cur = ext_ref[HALO:HALO + tt, cs]
        s = cur
        for i in range(1, w):
            s = s + ext_ref[HALO - i:HALO - i + tt, cs]
        d = s / jnp.minimum(n_seen, w).astype(F32) - cur
        outs.append(jnp.dot(d.astype(BF16), w_ref[g], preferred_element_type=F32))
    o_ref[...] = x + jnp.concatenate(outs, axis=-1) * s_ref[...]

    @pl.when(t == pl.num_programs(1) - 1)
    def _():
        nb_ref[0] = ext_ref[HALO + tt - POOL_BUF:HALO + tt, :]

    ext_ref[0:HALO, :] = ext_ref[tt:tt + HALO, :]


def _pool_sample_kernel(h_ref, buf_ref, g_ref, w_ref, s_ref, *rest, sb, t_len, n_carried):
    o_ref, nb_ref, ext_ref = rest[n_carried:]
    x = h_ref[...]
    u = _rmsnorm(x, g_ref[...])
    ext_ref[:, 1:HALO, :] = buf_ref[...]
    ext_ref[:, HALO:HALO + t_len, :] = u.reshape(sb, t_len, D_MODEL)
    outs = []
    for g, w in enumerate(POOL_WINDOWS):
        cs = slice(g * GROUP_DIM, (g + 1) * GROUP_DIM)
        cur = ext_ref[:, HALO:HALO + t_len, cs]
        s = cur
        for i in range(1, w):
            s = s + ext_ref[:, HALO - i:HALO - i + t_len, cs]
        d = (s * (1.0 / w) - cur).reshape(sb * t_len, GROUP_DIM)
        outs.append(jnp.dot(d.astype(BF16), w_ref[g], preferred_element_type=F32))
    o_ref[...] = x + jnp.concatenate(outs, axis=-1) * s_ref[...]
    nb_ref[...] = ext_ref[:, HALO + t_len - POOL_BUF:HALO + t_len, :]


def _pool_layer(j, layer, h, x_prompt, x_sample, state_pool, new_pool, gains, w_bf, scales, dims):
    batch, seq, dec_batch, t_len = dims
    n_prompt, n_tok = batch * seq, batch * seq + dec_batch * t_len
    first = h is None
    h_shape = jax.ShapeDtypeStruct((n_tok, D_MODEL), F32)
    w_spec = pl.BlockSpec((None, N_GROUPS, GROUP_DIM, GROUP_DIM), lambda *_: (j, 0, 0, 0))

    tt = 512
    nt = seq // tt
    row_spec = pl.BlockSpec((tt, D_MODEL), lambda b, t: (b * nt + t, 0))
    h, nb_p = _call(
        functools.partial(_pool_prompt_kernel, tt=tt, n_carried=0 if first else 1),
        name="pool_prompt", grid=(batch, nt), sem=("arbitrary", "arbitrary"),
        in_specs=[row_spec, _layer_row(layer), w_spec, _layer_row(j)],
        out_specs=[row_spec, pl.BlockSpec((None, 1, POOL_BUF, D_MODEL), lambda b, t: (j, b, 0, 0))],
        out_shape=[h_shape, jax.ShapeDtypeStruct((N_MIX, batch, POOL_BUF, D_MODEL), F32)],
        operands=(x_prompt if first else h, gains, w_bf, scales),
        carried=None if first else {1: new_pool[0]},
        aliases=None if first else {0: 0},
        scratch_shapes=[pltpu.VMEM((HALO + tt, D_MODEL), F32)])

    sb = 16
    rows = sb * t_len
    base = n_prompt // rows
    out_rows = pl.BlockSpec((rows, D_MODEL), lambda i: (base + i, 0))
    h, nb_s = _call(
        functools.partial(_pool_sample_kernel, sb=sb, t_len=t_len, n_carried=1),
        name="pool_sample", grid=(dec_batch // sb,), sem=("arbitrary",),
        in_specs=[pl.BlockSpec((rows, D_MODEL), lambda i: (i, 0)) if first else out_rows,
                  pl.BlockSpec((None, sb, POOL_BUF, D_MODEL), lambda i: (j, i, 0, 0)),
                  _layer_row(layer), w_spec, _layer_row(j)],
        out_specs=[out_rows, pl.BlockSpec((None, sb, POOL_BUF, D_MODEL), lambda i: (j, i, 0, 0))],
        out_shape=[h_shape, jax.ShapeDtypeStruct((N_MIX, dec_batch, POOL_BUF, D_MODEL), F32)],
        operands=(x_sample if first else h, state_pool, gains, w_bf, scales),
        carried={0: h} if first else {1: new_pool[1]},
        aliases=None if first else {0: 0},
        scratch_shapes=[pltpu.VMEM((sb, HALO + t_len, D_MODEL), F32)])
    return h, (nb_p, nb_s)


def _ffn_tile(x_ref, g_ref, wu_ref, wd_ref, o_ref, u_ref):
    @pl.when(pl.program_id(1) == 0)
    def _():
        x = x_ref[...]
        u_ref[...] = _rmsnorm(x, g_ref[...]).astype(BF16)
        o_ref[...] = x

    a = jnp.maximum(jnp.dot(u_ref[...], wu_ref[...], preferred_element_type=F32), 0.0)
    o_ref[...] += jnp.dot((a * a).astype(BF16), wd_ref[...], preferred_element_type=F32)


def _ffn_kernel(x_ref, g_ref, wu_ref, wd_ref, *rest, emit_u, cast_next):
    rest = list(rest)
    gn_ref = rest.pop(0) if emit_u else None
    wun_ref, wdn_ref = (rest.pop(0), rest.pop(0)) if cast_next else (None, None)
    o_ref = rest.pop(0)
    un_ref = rest.pop(0) if emit_u else None
    if cast_next:
        wun_bf_ref, wdn_bf_ref = rest.pop(0), rest.pop(0)
        wun_bf_ref[...] = wun_ref[...].astype(BF16)
        wdn_bf_ref[...] = wdn_ref[...].astype(BF16)
    (u_ref,) = rest
    _ffn_tile(x_ref, g_ref, wu_ref, wd_ref, o_ref, u_ref)

    if emit_u:
        @pl.when(pl.program_id(1) == pl.num_programs(1) - 1)
        def _():
            un_ref[...] = _rmsnorm(o_ref[...], gn_ref[...]).astype(BF16)


def _ffn_final_kernel(x_ref, g_ref, wu_ref, wd_ref, gf_ref, yp_ref, ys_ref, u_ref, *, n_prompt_tiles):
    def tile(o_ref):
        _ffn_tile(x_ref, g_ref, wu_ref, wd_ref, o_ref, u_ref)

        @pl.when(pl.program_id(1) == pl.num_programs(1) - 1)
        def _():
            o_ref[...] = _rmsnorm(o_ref[...], gf_ref[...])

    is_prompt = pl.program_id(0) < n_prompt_tiles
    pl.when(is_prompt)(lambda: tile(yp_ref))
    pl.when(jnp.logical_not(is_prompt))(lambda: tile(ys_ref))


FFN_TM, FFN_FC = 512, 1024
CAST_SPLIT = 16


def _ffn_layer(layer, h, gains, wu_bf, wd_bf, w_up_f32, w_down_f32, mix_gains, gain_final, n_prompt):
    n_tok = h.shape[0]
    tm, fc = FFN_TM, FFN_FC
    n_k = D_FF // fc
    row_spec = pl.BlockSpec((tm, D_MODEL), lambda i, k: (i, 0))
    in_specs = [row_spec, _layer_row(layer),
                pl.BlockSpec((D_MODEL, fc), lambda i, k: (0, k)),
                pl.BlockSpec((fc, D_MODEL), lambda i, k: (k, 0))]
    scratch = [pltpu.VMEM((tm, D_MODEL), BF16)]
    grid = (n_tok // tm, n_k)
    h_shape = jax.ShapeDtypeStruct((n_tok, D_MODEL), F32)
    if layer == DEPTH - 1:
        npt = n_prompt // tm
        ys = _call(
            functools.partial(_ffn_final_kernel, n_prompt_tiles=npt),
            name="ffn_final", grid=grid, sem=("arbitrary", "arbitrary"),
            in_specs=in_specs + [pl.BlockSpec((1, D_MODEL), lambda i, k: (0, 0))],
            out_specs=[pl.BlockSpec((tm, D_MODEL), lambda i, k: (jnp.minimum(i, npt - 1), 0)),
                       pl.BlockSpec((tm, D_MODEL), lambda i, k: (jnp.maximum(i - npt, 0), 0))],
            out_shape=[jax.ShapeDtypeStruct((n_prompt, D_MODEL), F32),
                       jax.ShapeDtypeStruct((n_tok - n_prompt, D_MODEL), F32)],
            operands=(h, gains, wu_bf, wd_bf, gain_final), scratch_shapes=scratch)
        return ys, None, None

    emit_u = layer % 2 == 0
    cast_blk = lambda i, k: (jnp.minimum(i, CAST_SPLIT - 1), jnp.where(i < CAST_SPLIT, k, n_k - 1))
    up_blk, down_blk = (D_MODEL // CAST_SPLIT, fc), (D_FF // CAST_SPLIT, D_MODEL // n_k)
    operands, out_specs, out_shape = [h, gains, wu_bf, wd_bf], [row_spec], [h_shape]
    if emit_u:
        in_specs.append(_layer_row(layer + 1))
        operands.append(mix_gains)
        out_specs.append(row_spec)
        out_shape.append(jax.ShapeDtypeStruct((n_tok, D_MODEL), BF16))
    in_specs += [pl.BlockSpec((None,) + up_blk, lambda i, k: (layer + 1,) + cast_blk(i, k)),
                 pl.BlockSpec((None,) + down_blk, lambda i, k: (layer + 1,) + cast_blk(i, k))]
    operands += [w_up_f32, w_down_f32]
    out_specs += [pl.BlockSpec(up_blk, cast_blk), pl.BlockSpec(down_blk, cast_blk)]
    out_shape += [jax.ShapeDtypeStruct((D_MODEL, D_FF), BF16), jax.ShapeDtypeStruct((D_FF, D_MODEL), BF16)]
    outs = _call(functools.partial(_ffn_kernel, emit_u=emit_u, cast_next=True),
                 name="ffn", grid=grid, sem=("arbitrary", "arbitrary"),
                 in_specs=in_specs, out_specs=out_specs, out_shape=out_shape,
                 operands=tuple(operands), scratch_shapes=scratch)
    return outs[0], (outs[1] if emit_u else None), tuple(outs[-2:])


_NT = (((1,), (1,)), ((), ()))


def _mlstm_in_kernel(u_ref, w_ref, wg_ref, bg_ref, p_ref, gc_ref, w_scr):
    jj, i = pl.program_id(0), pl.program_id(1)

    @pl.when(i == 0)
    def _():
        w_scr[...] = w_ref[...].astype(BF16)

    u = u_ref[...]
    p_ref[...] = lax.dot_general(u, w_scr[...], _NT, preferred_element_type=F32).astype(BF16)

    @pl.when(jj == 0)
    def _():
        pre = lax.dot_general(u, wg_ref[...].astype(BF16), _NT, preferred_element_type=F32) + bg_ref[...]
        capped = GATE_CAP * jnp.tanh(pre / GATE_CAP)
        log_f = jnp.minimum(capped, 0.0) - jnp.log1p(jnp.exp(-jnp.abs(capped)))
        lane = lax.broadcasted_iota(jnp.int32, capped.shape, 1)
        gc_ref[...] = jnp.where(lane < HEADS, capped, log_f)


def _mlstm_in(j, u, w_t, wg_t, b_gate):
    n_tok = u.shape[0]
    tm, nb = 1024, 1024
    n_i = n_tok // tm
    return _call(
        _mlstm_in_kernel, name="mlstm_in", grid=(QKVO_DIM // nb, n_i), sem=("arbitrary", "arbitrary"),
        in_specs=[pl.BlockSpec((tm, D_MODEL), lambda jj, i: (i, 0)),
                  pl.BlockSpec((None, nb, D_MODEL), lambda jj, i: (j, jj, 0)),
                  pl.BlockSpec((None, LANES, D_MODEL), lambda jj, i: (j, 0, 0)),
                  _layer_row(j, LANES)],
        out_specs=[pl.BlockSpec((tm, nb), lambda jj, i: (i, jj)),
                   pl.BlockSpec((tm, LANES), lambda jj, i: (jnp.where(jj == 0, i, n_i - 1), 0))],
        out_shape=[jax.ShapeDtypeStruct((n_tok, QKVO_DIM), BF16),
                   jax.ShapeDtypeStruct((n_tok, LANES), F32)],
        operands=(u, w_t, wg_t, b_gate),
        scratch_shapes=[pltpu.VMEM((nb, D_MODEL), BF16)])


def _dot_f32(lhs, rhs, dims):
    return lax.dot_general(lhs, rhs, (dims, ((), ())), preferred_element_type=F32,
                           precision=lax.Precision.HIGHEST)


def _mlstm_rec_kernel(q_ref, k_ref, v_ref, o_ref, gc_ref, hn_ref, *rest, chunk, n_sub, zero_init, single_chunk,
                      n_carried):
    if zero_init:
        c0_ref = n0_ref = m0_ref = None
    else:
        c0_ref, n0_ref, m0_ref = rest[:3]
        rest = rest[3:]
    y_ref, c_ref, n_ref, m_ref = rest[n_carried:]

    if single_chunk and not zero_init:
        c_in, n_in, m_in = c0_ref, n0_ref, m0_ref
    else:
        c_in, n_in, m_in = c_ref, n_ref, m_ref

        @pl.when(pl.program_id(1) == 0)
        def _():
            if zero_init:
                c_ref[...] = jnp.zeros(c_ref.shape, F32)
                n_ref[...] = jnp.zeros(n_ref.shape, F32)
                m_ref[...] = jnp.zeros(m_ref.shape, F32)
            else:
                c_ref[...] = c0_ref[...]
                n_ref[...] = n0_ref[...]
                m_ref[...] = m0_ref[...]

    head_row = lax.broadcasted_iota(jnp.int32, (HEADS, DK), 0)
    head_lane = lax.broadcasted_iota(jnp.int32, (1, LANES), 1)

    row = lax.broadcasted_iota(jnp.int32, (chunk, chunk), 0)
    col = lax.broadcasted_iota(jnp.int32, (chunk, chunk), 1)
    causal = col <= row
    pick = (lax.broadcasted_iota(jnp.int32, (2 * HEADS, LANES), 0)
            == lax.broadcasted_iota(jnp.int32, (2 * HEADS, LANES), 1)).astype(F32)

    for sub in range(n_sub):
        rows = slice(sub * chunk, (sub + 1) * chunk)
        gates_c = gc_ref[rows, :]
        gates_r = _dot_f32(pick, gates_c, ((1,), (1,)))
        cum_c = _dot_f32(causal.astype(F32), gates_c, ((1,), (0,)))
        cum_r = _dot_f32(gates_r, (row <= col).astype(F32), ((1,), (0,)))
        n_all = n_in[sub]
        m_all = m_in[sub]
        n_out = jnp.zeros((HEADS, DK), F32)
        m_out = jnp.zeros((1, LANES), F32)

        for hd in range(HEADS):
            q = q_ref[rows, hd * DK:(hd + 1) * DK] * BF16(DK ** -0.5)
            k = k_ref[rows, hd * DK:(hd + 1) * DK]
            v = v_ref[rows, hd * DV:(hd + 1) * DV]
            i_col = gates_c[:, hd:hd + 1]
            i_row = gates_r[hd:hd + 1, :]
            b_col = cum_c[:, HEADS + hd:HEADS + hd + 1]
            b_row = cum_r[HEADS + hd:HEADS + hd + 1, :]
            c_prev = c_in[sub, hd]
            n_prev = n_all[hd:hd + 1, :]
            m_prev = m_all[:, hd:hd + 1]

            a = b_col + m_prev
            dm = jnp.where(causal, b_col - b_row + i_row, -jnp.inf)
            m_row = jnp.maximum(a, jnp.max(dm, axis=1, keepdims=True))
            w_intra = jnp.exp(dm - m_row)
            w_inter = jnp.exp(a - m_row)
            s = lax.dot_general(q, k, (((1,), (1,)), ((), ())), preferred_element_type=F32) * w_intra
            num = (w_inter * jnp.dot(q, c_prev.astype(BF16), preferred_element_type=F32)
                   + jnp.dot(s.astype(BF16), v, preferred_element_type=F32))
            den = (w_inter * jnp.sum(q.astype(F32) * n_prev, axis=1, keepdims=True)
                   + jnp.sum(s, axis=1, keepdims=True))
            hh = num / jnp.maximum(jnp.abs(den), jnp.exp(-m_row))

            b_last = b_col[chunk - 1:chunk, :]
            m_new = m_row[chunk - 1:chunk, :]
            decay = jnp.exp(b_last + m_prev - m_new)
            kw = k.astype(F32) * jnp.exp(b_last - b_col + i_col - m_new)
            c_ref[sub, hd] = decay * c_prev + lax.dot_general(
                kw.astype(BF16), v, (((0,), (0,)), ((), ())), preferred_element_type=F32)
            n_new = decay * n_prev + jnp.sum(kw, axis=0, keepdims=True)
            n_out = jnp.where(head_row == hd, n_new, n_out)
            m_out = jnp.where(head_lane == hd, m_new, m_out)

            hh = hh * lax.rsqrt(jnp.mean(hh * hh, axis=-1, keepdims=True) + EPS)
            hh = hh * hn_ref[:, hd * DV:(hd + 1) * DV]
            o = o_ref[rows, hd * DV:(hd + 1) * DV].astype(F32)
            y_ref[rows, hd * DV:(hd + 1) * DV] = (jax.nn.sigmoid(o) * hh).astype(BF16)

        n_ref[sub] = n_out
        m_ref[sub] = m_out


def _mlstm_rec(j, proj, gates_c, head_norm, state, y, new_state, *, first_row, n_seq, t_len, chunk, n_sub):
    n_tok = proj.shape[0]
    n_chunks = t_len // chunk
    blk = n_sub * chunk
    base = first_row // blk
    zero_init = state is None
    rows = lambda width_blk: (lambda b, c: (base + b * n_chunks + c, width_blk))
    st_specs = [pl.BlockSpec((None, n_sub, HEADS, DK, DV), lambda b, c: (j, b, 0, 0, 0)),
                pl.BlockSpec((None, n_sub, HEADS, DK), lambda b, c: (j, b, 0, 0)),
                pl.BlockSpec((None, n_sub, 1, LANES), lambda b, c: (j, b, 0, 0))]
    carried = {}
    if y is not None:
        carried[0] = y
    if new_state is not None:
        carried.update({1: new_state[0], 2: new_state[1], 3: new_state[2]})
    outs = _call(
        functools.partial(_mlstm_rec_kernel, chunk=chunk, n_sub=n_sub, zero_init=zero_init,
                          single_chunk=(n_chunks == 1), n_carried=len(carried)),
        name="mlstm_rec_prompt" if zero_init else "mlstm_rec_sample",
        grid=(n_seq // n_sub, n_chunks), sem=("arbitrary", "arbitrary"),
        in_specs=[pl.BlockSpec((blk, QK_DIM), rows(0)),
                  pl.BlockSpec((blk, QK_DIM), rows(1)),
                  pl.BlockSpec((blk, V_DIM), rows(1)),
                  pl.BlockSpec((blk, V_DIM), rows(2)),
                  pl.BlockSpec((blk, LANES), rows(0)),
                  _layer_row(j)] + ([] if zero_init else st_specs),
        out_specs=[pl.BlockSpec((blk, V_DIM), rows(0))] + st_specs,
        out_shape=[jax.ShapeDtypeStruct((n_tok, V_DIM), BF16),
                   jax.ShapeDtypeStruct((N_MIX, n_seq, HEADS, DK, DV), F32),
                   jax.ShapeDtypeStruct((N_MIX, n_seq, HEADS, DK), F32),
                   jax.ShapeDtypeStruct((N_MIX, n_seq, 1, LANES), F32)],
        operands=(proj, proj, proj, proj, gates_c, head_norm) + (() if zero_init else tuple(state)),
        carried=carried)
    return outs[0], tuple(outs[1:])


def _out_proj_kernel(x_ref, y_ref, w_ref, o_ref):
    o_ref[...] = x_ref[...] + jnp.dot(y_ref[...], w_ref[...], preferred_element_type=F32)


def _out_proj(j, h, y, w_bf):
    n_tok = h.shape[0]
    tm = 512
    row_spec = pl.BlockSpec((tm, D_MODEL), lambda i: (i, 0))
    return _call(
        _out_proj_kernel, name="mlstm_out", grid=(n_tok // tm,), sem=("parallel",),
        in_specs=[row_spec, pl.BlockSpec((tm, V_DIM), lambda i: (i, 0)),
                  pl.BlockSpec((None, V_DIM, D_MODEL), lambda i: (j, 0, 0))],
        out_specs=row_spec, out_shape=jax.ShapeDtypeStruct((n_tok, D_MODEL), F32),
        operands=(h, y, w_bf))


def _mlstm_layer(j, h, u, state_s, new_p, new_s, w_t, wg_t, b_gate, head_norm, w_out_bf, dims):
    batch, seq, dec_batch, t_len = dims
    proj, gates_c = _mlstm_in(j, u, w_t, wg_t, b_gate)
    y, new_p = _mlstm_rec(j, proj, gates_c, head_norm, None, None, new_p,
                          first_row=0, n_seq=batch, t_len=seq, chunk=256, n_sub=1)
    y, new_s = _mlstm_rec(j, proj, gates_c, head_norm, state_s, y, new_s,
                          first_row=batch * seq, n_seq=dec_batch, t_len=t_len, chunk=t_len, n_sub=4)
    return _out_proj(j, h, y, w_out_bf), new_p, new_s


def kernel(x_prompt, x_sample, state_pool, state_mlstm_C, state_mlstm_n, state_mlstm_m, norm_mix, norm_ffn, norm_final, pool_w, pool_scale, mlstm_w_in, mlstm_b_gate, mlstm_head_norm, mlstm_w_out, ffn_w_up, ffn_w_down):
    batch, seq, _ = x_prompt.shape
    dec_batch, t_len, _ = x_sample.shape
    dims = (batch, seq, dec_batch, t_len)
    n_prompt = batch * seq
    x_prompt = x_prompt.reshape(n_prompt, D_MODEL)
    x_sample = x_sample.reshape(dec_batch * t_len, D_MODEL)

    rows3 = lambda a: a.reshape(a.shape[0], 1, a.shape[1])
    norm_mix, norm_ffn, pool_scale, head_norm = map(rows3, (norm_mix, norm_ffn, pool_scale, mlstm_head_norm))
    pad_lanes = lambda a: jnp.pad(a, [(0, 0)] * (a.ndim - 1) + [(0, LANES - a.shape[-1])])
    b_gate = rows3(pad_lanes(mlstm_b_gate))
    state_s = (state_mlstm_C, state_mlstm_n, pad_lanes(state_mlstm_m)[:, :, None, :])
    pool_w_bf = pool_w.astype(BF16)
    w_in_t = jnp.swapaxes(mlstm_w_in, 1, 2)
    w_gate_t = jnp.pad(w_in_t[:, QKVO_DIM:, :], ((0, 0), (0, LANES - 2 * HEADS), (0, 0)))
    w_out_bf = mlstm_w_out.astype(BF16)
    ffn_bf = (ffn_w_up[0].astype(BF16), ffn_w_down[0].astype(BF16))
    norm_final = norm_final.reshape(1, D_MODEL)

    h = u = new_pool = new_p = new_s = None
    for layer in range(DEPTH):
        j = layer // 2
        if layer % 2 == 0:
            h, new_pool = _pool_layer(j, layer, h, x_prompt, x_sample, state_pool, new_pool,
                                      norm_mix, pool_w_bf, pool_scale, dims)
        else:
            h, new_p, new_s = _mlstm_layer(j, h, u, state_s, new_p, new_s, w_in_t, w_gate_t,
                                           b_gate, head_norm, w_out_bf, dims)
        h, u, ffn_bf = _ffn_layer(layer, h, norm_ffn, *ffn_bf, ffn_w_up, ffn_w_down, norm_mix, norm_final,
                                  n_prompt)

    y_prompt = h[0].reshape(batch, seq, D_MODEL)
    y_sample = h[1].reshape(dec_batch, t_len, D_MODEL)
    return (y_prompt, y_sample, new_pool[0], new_pool[1],
            new_p[0], new_p[1], new_p[2][:, :, 0, :HEADS],
            new_s[0], new_s[1], new_s[2][:, :, 0, :HEADS])
```

```python
import functools

import jax
import jax.numpy as jnp
from jax import lax
from jax.experimental import pallas as pl
from jax.experimental.pallas import tpu as pltpu

F32 = jnp.float32
BF16 = jnp.bfloat16

D_MODEL = 2048
DEPTH = 4
N_MIX = DEPTH // 2
POOL_WINDOWS = (2, 4, 8, 16)
N_GROUPS = len(POOL_WINDOWS)
GROUP_DIM = D_MODEL // N_GROUPS
POOL_BUF = max(POOL_WINDOWS) - 1
HALO = POOL_BUF + 1
HEADS = 4
QK_DIM = D_MODEL // 2
V_DIM = D_MODEL
DK = QK_DIM // HEADS
DV = V_DIM // HEADS
QKVO_DIM = 2 * QK_DIM + 2 * V_DIM
GATE_CAP = 15.0
D_FF = 4 * D_MODEL
EPS = 1e-6
LANES = 128

VMEM_LIMIT = 56 * 1024 * 1024


def _rmsnorm(x, g):
    return x * lax.rsqrt(jnp.mean(x * x, axis=-1, keepdims=True) + EPS) * g


def _call(body, *, name, grid, sem, in_specs, out_specs, out_shape, operands, carried=None, aliases=None,
          scratch_shapes=()):
    carried = carried or {}
    aliases = dict(aliases or {})
    for pos, out_idx in enumerate(carried):
        aliases[len(operands) + pos] = out_idx
    return pl.pallas_call(
        body,
        grid=grid,
        in_specs=list(in_specs) + [pl.BlockSpec(memory_space=pl.ANY)] * len(carried),
        out_specs=out_specs,
        out_shape=out_shape,
        scratch_shapes=list(scratch_shapes),
        input_output_aliases=aliases,
        compiler_params=pltpu.CompilerParams(dimension_semantics=sem, vmem_limit_bytes=VMEM_LIMIT),
        name=name,
    )(*operands, *carried.values())


def _layer_row(layer, d=D_MODEL):
    return pl.BlockSpec((None, 1, d), lambda *_: (layer, 0, 0))


def _pool_prompt_kernel(h_ref, g_ref, w_ref, s_ref, *rest, tt, n_carried):
    o_ref, nb_ref, ext_ref = rest[n_carried:]
    t = pl.program_id(1)

    @pl.when(t == 0)
    def _():
        ext_ref[0:HALO, :] = jnp.zeros((HALO, D_MODEL), F32)

    x = h_ref[...]
    ext_ref[HALO:HALO + tt, :] = _rmsnorm(x, g_ref[...])
    n_seen = t * tt + 1 + lax.broadcasted_iota(jnp.int32, (tt, 1), 0)
    outs = []
    for g, w in enumerate(POOL_WINDOWS):
        cs = slice(g * GROUP_DIM, (g + 1) * GROUP_DIM)
        cur = ext_ref[HALO:HALO + tt, cs]
        s = cur
        for i in range(1, w):
            s = s + ext_ref[HALO - i:HALO - i + tt, cs]
        d = s / jnp.minimum(n_seen, w).astype(F32) - cur
        outs.append(jnp.dot(d.astype(BF16), w_ref[g], preferred_element_type=F32))
    o_ref[...] = x + jnp.concatenate(outs, axis=-1) * s_ref[...]

    @pl.when(t == pl.num_programs(1) - 1)
    def _():
        nb_ref[0] = ext_ref[HALO + tt - POOL_BUF:HALO + tt, :]

    ext_ref[0:HALO, :] = ext_ref[tt:tt + HALO, :]


def _pool_sample_kernel(h_ref, buf_ref, g_ref, w_ref, s_ref, *rest, sb, t_len, n_carried):
    o_ref, nb_ref, ext_ref = rest[n_carried:]
    x = h_ref[...]
    u = _rmsnorm(x, g_ref[...])
    ext_ref[:, 1:HALO, :] = buf_ref[...]
    ext_ref[:, HALO:HALO + t_len, :] = u.reshape(sb, t_len, D_MODEL)
    outs = []
    for g, w in enumerate(POOL_WINDOWS):
        cs = slice(g * GROUP_DIM, (g + 1) * GROUP_DIM)
        cur = ext_ref[:, HALO:HALO + t_len, cs]
        s = cur
        for i in range(1, w):
            s = s + ext_ref[:, HALO - i:HALO - i + t_len, cs]
        d = (s * (1.0 / w) - cur).reshape(sb * t_len, GROUP_DIM)
        outs.append(jnp.dot(d.astype(BF16), w_ref[g], preferred_element_type=F32))
    o_ref[...] = x + jnp.concatenate(outs, axis=-1) * s_ref[...]
    nb_ref[...] = ext_ref[:, HALO + t_len - POOL_BUF:HALO + t_len, :]


def _pool_layer(j, layer, h, x_prompt, x_sample, state_pool, new_pool, gains, w_bf, scales, dims):
    batch, seq, dec_batch, t_len = dims
    n_prompt, n_tok = batch * seq, batch * seq + dec_batch * t_len
    first = h is None
    h_shape = jax.ShapeDtypeStruct((n_tok, D_MODEL), F32)
    w_spec = pl.BlockSpec((None, N_GROUPS, GROUP_DIM, GROUP_DIM), lambda *_: (j, 0, 0, 0))

    tt = 512
    nt = seq // tt
    row_spec = pl.BlockSpec((tt, D_MODEL), lambda b, t: (b * nt + t, 0))
    h, nb_p = _call(
        functools.partial(_pool_prompt_kernel, tt=tt, n_carried=0 if first else 1),
        name="pool_prompt", grid=(batch, nt), sem=("arbitrary", "arbitrary"),
        in_specs=[row_spec, _layer_row(layer), w_spec, _layer_row(j)],
        out_specs=[row_spec, pl.BlockSpec((None, 1, POOL_BUF, D_MODEL), lambda b, t: (j, b, 0, 0))],
        out_shape=[h_shape, jax.ShapeDtypeStruct((N_MIX, batch, POOL_BUF, D_MODEL), F32)],
        operands=(x_prompt if first else h, gains, w_bf, scales),
        carried=None if first else {1: new_pool[0]},
        aliases=None if first else {0: 0},
        scratch_shapes=[pltpu.VMEM((HALO + tt, D_MODEL), F32)])

    sb = 16
    rows = sb * t_len
    base = n_prompt // rows
    out_rows = pl.BlockSpec((rows, D_MODEL), lambda i: (base + i, 0))
    h, nb_s = _call(
        functools.partial(_pool_sample_kernel, sb=sb, t_len=t_len, n_carried=1),
        name="pool_sample", grid=(dec_batch // sb,), sem=("arbitrary",),
        in_specs=[pl.BlockSpec((rows, D_MODEL), lambda i: (i, 0)) if first else out_rows,
                  pl.BlockSpec((None, sb, POOL_BUF, D_MODEL), lambda i: (j, i, 0, 0)),
                  _layer_row(layer), w_spec, _layer_row(j)],
        out_specs=[out_rows, pl.BlockSpec((None, sb, POOL_BUF, D_MODEL), lambda i: (j, i, 0, 0))],
        out_shape=[h_shape, jax.ShapeDtypeStruct((N_MIX, dec_batch, POOL_BUF, D_MODEL), F32)],
        operands=(x_sample if first else h, state_pool, gains, w_bf, scales),
        carried={0: h} if first else {1: new_pool[1]},
        aliases=None if first else {0: 0},
        scratch_shapes=[pltpu.VMEM((sb, HALO + t_len, D_MODEL), F32)])
    return h, (nb_p, nb_s)


def _ffn_tile(x_ref, g_ref, wu_ref, wd_ref, o_ref, u_ref):
    @pl.when(pl.program_id(1) == 0)
    def _():
        x = x_ref[...]
        u_ref[...] = _rmsnorm(x, g_ref[...]).astype(BF16)
        o_ref[...] = x

    a = jnp.maximum(jnp.dot(u_ref[...], wu_ref[...], preferred_element_type=F32), 0.0)
    o_ref[...] += jnp.dot((a * a).astype(BF16), wd_ref[...], preferred_element_type=F32)


def _ffn_kernel(x_ref, g_ref, wu_ref, wd_ref, *rest, emit_u, cast_next):
    rest = list(rest)
    gn_ref = rest.pop(0) if emit_u else None
    wun_ref, wdn_ref = (rest.pop(0), rest.pop(0)) if cast_next else (None, None)
    o_ref = rest.pop(0)
    un_ref = rest.pop(0) if emit_u else None
    if cast_next:
        wun_bf_ref, wdn_bf_ref = rest.pop(0), rest.pop(0)
        wun_bf_ref[...] = wun_ref[...].astype(BF16)
        wdn_bf_ref[...] = wdn_ref[...].astype(BF16)
    (u_ref,) = rest
    _ffn_tile(x_ref, g_ref, wu_ref, wd_ref, o_ref, u_ref)

    if emit_u:
        @pl.when(pl.program_id(1) == pl.num_programs(1) - 1)
        def _():
            un_ref[...] = _rmsnorm(o_ref[...], gn_ref[...]).astype(BF16)


def _ffn_final_kernel(x_ref, g_ref, wu_ref, wd_ref, gf_ref, yp_ref, ys_ref, u_ref, *, n_prompt_tiles):
    def tile(o_ref):
        _ffn_tile(x_ref, g_ref, wu_ref, wd_ref, o_ref, u_ref)

        @pl.when(pl.program_id(1) == pl.num_programs(1) - 1)
        def _():
            o_ref[...] = _rmsnorm(o_ref[...], gf_ref[...])

    is_prompt = pl.program_id(0) < n_prompt_tiles
    pl.when(is_prompt)(lambda: tile(yp_ref))
    pl.when(jnp.logical_not(is_prompt))(lambda: tile(ys_ref))


FFN_TM, FFN_FC = 512, 1024
CAST_SPLIT = 16


def _ffn_layer(layer, h, gains, wu_bf, wd_bf, w_up_f32, w_down_f32, mix_gains, gain_final, n_prompt):
    n_tok = h.shape[0]
    tm, fc = FFN_TM, FFN_FC
    n_k = D_FF // fc
    row_spec = pl.BlockSpec((tm, D_MODEL), lambda i, k: (i, 0))
    in_specs = [row_spec, _layer_row(layer),
                pl.BlockSpec((D_MODEL, fc), lambda i, k: (0, k)),
                pl.BlockSpec((fc, D_MODEL), lambda i, k: (k, 0))]
    scratch = [pltpu.VMEM((tm, D_MODEL), BF16)]
    grid = (n_tok // tm, n_k)
    h_shape = jax.ShapeDtypeStruct((n_tok, D_MODEL), F32)
    if layer == DEPTH - 1:
        npt = n_prompt // tm
        ys = _call(
            functools.partial(_ffn_final_kernel, n_prompt_tiles=npt),
            name="ffn_final", grid=grid, sem=("arbitrary", "arbitrary"),
            in_specs=in_specs + [pl.BlockSpec((1, D_MODEL), lambda i, k: (0, 0))],
            out_specs=[pl.BlockSpec((tm, D_MODEL), lambda i, k: (jnp.minimum(i, npt - 1), 0)),
                       pl.BlockSpec((tm, D_MODEL), lambda i, k: (jnp.maximum(i - npt, 0), 0))],
            out_shape=[jax.ShapeDtypeStruct((n_prompt, D_MODEL), F32),
                       jax.ShapeDtypeStruct((n_tok - n_prompt, D_MODEL), F32)],
            operands=(h, gains, wu_bf, wd_bf, gain_final), scratch_shapes=scratch)
        return ys, None, None

    emit_u = layer % 2 == 0
    cast_blk = lambda i, k: (jnp.minimum(i, CAST_SPLIT - 1), jnp.where(i < CAST_SPLIT, k, n_k - 1))
    up_blk, down_blk = (D_MODEL // CAST_SPLIT, fc), (D_FF // CAST_SPLIT, D_MODEL // n_k)
    operands, out_specs, out_shape = [h, gains, wu_bf, wd_bf], [row_spec], [h_shape]
    if emit_u:
        in_specs.append(_layer_row(layer + 1))
        operands.append(mix_gains)
        out_specs.append(row_spec)
        out_shape.append(jax.ShapeDtypeStruct((n_tok, D_MODEL), BF16))
    in_specs += [pl.BlockSpec((None,) + up_blk, lambda i, k: (layer + 1,) + cast_blk(i, k)),
                 pl.BlockSpec((None,) + down_blk, lambda i, k: (layer + 1,) + cast_blk(i, k))]
    operands += [w_up_f32, w_down_f32]
    out_specs += [pl.BlockSpec(up_blk, cast_blk), pl.BlockSpec(down_blk, cast_blk)]
    out_shape += [jax.ShapeDtypeStruct((D_MODEL, D_FF), BF16), jax.ShapeDtypeStruct((D_FF, D_MODEL), BF16)]
    outs = _call(functools.partial(_ffn_kernel, emit_u=emit_u, cast_next=True),
                 name="ffn", grid=grid, sem=("arbitrary", "arbitrary"),
                 in_specs=in_specs, out_specs=out_specs, out_shape=out_shape,
                 operands=tuple(operands), scratch_shapes=scratch)
    return outs[0], (outs[1] if emit_u else None), tuple(outs[-2:])


_NT = (((1,), (1,)), ((), ()))


def _mlstm_in_kernel(u_ref, w_ref, wg_ref, bg_ref, p_ref, gc_ref, w_scr):
    jj, i = pl.program_id(0), pl.program_id(1)

    @pl.when(i == 0)
    def _():
        w_scr[...] = w_ref[...].astype(BF16)

    u = u_ref[...]
    p_ref[...] = lax.dot_general(u, w_scr[...], _NT, preferred_element_type=F32).astype(BF16)

    @pl.when(jj == 0)
    def _():
        pre = lax.dot_general(u, wg_ref[...].astype(BF16), _NT, preferred_element_type=F32) + bg_ref[...]
        capped = GATE_CAP * jnp.tanh(pre / GATE_CAP)
        log_f = jnp.minimum(capped, 0.0) - jnp.log1p(jnp.exp(-jnp.abs(capped)))
        lane = lax.broadcasted_iota(jnp.int32, capped.shape, 1)
        gc_ref[...] = jnp.where(lane < HEADS, capped, log_f)


def _mlstm_in(j, u, w_t, wg_t, b_gate):
    n_tok = u.shape[0]
    tm, nb = 1024, 1024
    n_i = n_tok // tm
    return _call(
        _mlstm_in_kernel, name="mlstm_in", grid=(QKVO_DIM // nb, n_i), sem=("arbitrary", "arbitrary"),
        in_specs=[pl.BlockSpec((tm, D_MODEL), lambda jj, i: (i, 0)),
                  pl.BlockSpec((None, nb, D_MODEL), lambda jj, i: (j, jj, 0)),
                  pl.BlockSpec((None, LANES, D_MODEL), lambda jj, i: (j, 0, 0)),
                  _layer_row(j, LANES)],
        out_specs=[pl.BlockSpec((tm, nb), lambda jj, i: (i, jj)),
                   pl.BlockSpec((tm, LANES), lambda jj, i: (jnp.where(jj == 0, i, n_i - 1), 0))],
        out_shape=[jax.ShapeDtypeStruct((n_tok, QKVO_DIM), BF16),
                   jax.ShapeDtypeStruct((n_tok, LANES), F32)],
        operands=(u, w_t, wg_t, b_gate),
        scratch_shapes=[pltpu.VMEM((nb, D_MODEL), BF16)])


def _dot_f32(lhs, rhs, dims):
    return lax.dot_general(lhs, rhs, (dims, ((), ())), preferred_element_type=F32,
                           precision=lax.Precision.HIGHEST)


_TN = (((0,), (0,)), ((), ()))


def _gated_heads(num, den, m_row, o, hn):
    hh = num / jnp.maximum(jnp.abs(den), jnp.exp(-m_row))
    hh = hh * lax.rsqrt(jnp.mean(hh * hh, axis=-1, keepdims=True) + EPS)
    hh = hh * hn
    return (jax.nn.sigmoid(o.astype(F32)) * hh).astype(BF16)


def _gate_sums(gates_c, causal, causal_t):
    pick = (lax.broadcasted_iota(jnp.int32, (2 * HEADS, LANES), 0)
            == lax.broadcasted_iota(jnp.int32, (2 * HEADS, LANES), 1)).astype(F32)
    gates_r = _dot_f32(pick, gates_c, ((1,), (1,)))
    cum_c = _dot_f32(causal.astype(F32), gates_c, ((1,), (0,)))
    cum_r = _dot_f32(gates_r, causal_t.astype(F32), ((1,), (0,)))
    return gates_r, cum_c, cum_r


def _mlstm_prompt_kernel(q_ref, k_ref, v_ref, o_ref, gc_ref, hn_ref, *rest, chunk, n_carried):
    y_ref, c_ref, n_ref, m_ref = rest[n_carried:]

    @pl.when(pl.program_id(1) == 0)
    def _():
        c_ref[...] = jnp.zeros(c_ref.shape, F32)
        n_ref[...] = jnp.zeros(n_ref.shape, F32)
        m_ref[...] = jnp.zeros(m_ref.shape, F32)

    head_row = lax.broadcasted_iota(jnp.int32, (HEADS, DK), 0)
    head_lane = lax.broadcasted_iota(jnp.int32, (1, LANES), 1)
    row = lax.broadcasted_iota(jnp.int32, (chunk, chunk), 0)
    col = lax.broadcasted_iota(jnp.int32, (chunk, chunk), 1)
    causal = col <= row
    gates_c = gc_ref[...]
    gates_r, cum_c, cum_r = _gate_sums(gates_c, causal, row <= col)
    n_all = n_ref[0]
    m_all = m_ref[0]
    n_out = jnp.zeros((HEADS, DK), F32)
    m_out = jnp.zeros((1, LANES), F32)

    for hd in range(HEADS):
        q = q_ref[:, hd * DK:(hd + 1) * DK] * BF16(DK ** -0.5)
        k = k_ref[:, hd * DK:(hd + 1) * DK]
        v = v_ref[:, hd * DV:(hd + 1) * DV]
        i_col = gates_c[:, hd:hd + 1]
        i_row = gates_r[hd:hd + 1, :]
        b_col = cum_c[:, HEADS + hd:HEADS + hd + 1]
        b_row = cum_r[HEADS + hd:HEADS + hd + 1, :]
        c_prev = c_ref[0, hd]
        m_prev = m_all[:, hd:hd + 1]

        a = b_col + m_prev
        dm = jnp.where(causal, b_col - b_row + i_row, -jnp.inf)
        m_row = jnp.maximum(a, jnp.max(dm, axis=1, keepdims=True))
        w_intra = jnp.exp(dm - m_row)
        w_inter = jnp.exp(a - m_row)
        s = lax.dot_general(q, k, _NT, preferred_element_type=F32) * w_intra
        num = (w_inter * jnp.dot(q, c_prev.astype(BF16), preferred_element_type=F32)
               + jnp.dot(s.astype(BF16), v, preferred_element_type=F32))
        den = (w_inter * jnp.sum(q.astype(F32) * n_all[hd:hd + 1, :], axis=1, keepdims=True)
               + jnp.sum(s, axis=1, keepdims=True))

        b_last = b_col[chunk - 1:chunk, :]
        m_new = m_row[chunk - 1:chunk, :]
        decay = jnp.exp(b_last + m_prev - m_new)
        kw = k.astype(F32) * jnp.exp(b_last - b_col + i_col - m_new)
        c_ref[0, hd] = decay * c_prev + lax.dot_general(kw.astype(BF16), v, _TN, preferred_element_type=F32)
        n_new = decay * n_all[hd:hd + 1, :] + jnp.sum(kw, axis=0, keepdims=True)
        n_out = jnp.where(head_row == hd, n_new, n_out)
        m_out = jnp.where(head_lane == hd, m_new, m_out)

        y_ref[:, hd * DV:(hd + 1) * DV] = _gated_heads(
            num, den, m_row, o_ref[:, hd * DV:(hd + 1) * DV], hn_ref[:, hd * DV:(hd + 1) * DV])

    n_ref[0] = n_out
    m_ref[0] = m_out


def _mlstm_sample_kernel(q_ref, k_ref, v_ref, o_ref, gc_ref, hn_ref, c0_ref, n0_ref, m0_ref, *rest,
                         t_len, n_sub, n_carried):
    y_ref, c_ref, n_ref, m_ref = rest[n_carried:]
    n_rows = n_sub * t_len
    row = lax.broadcasted_iota(jnp.int32, (n_rows, n_rows), 0)
    col = lax.broadcasted_iota(jnp.int32, (n_rows, n_rows), 1)
    seq_start = lax.broadcasted_iota(jnp.int32, (n_sub, t_len, n_rows), 0).reshape(n_rows, n_rows) * t_len
    same_seq = (col >= seq_start) & (col < seq_start + t_len)
    causal = same_seq & (col <= row)
    causal_t = same_seq & (row <= col)

    def per_seq(x):
        return jnp.broadcast_to(x, (n_sub, t_len, x.shape[-1])).reshape(n_rows, x.shape[-1])

    def last_token(x):
        return per_seq(x.reshape(n_sub, t_len, x.shape[-1])[:, t_len - 1:t_len, :])

    gates_c = gc_ref[...]
    gates_r, cum_c, cum_r = _gate_sums(gates_c, causal, causal_t)
    m_prev_all = per_seq(m0_ref[...])
    b_last_all = last_token(cum_c)
    head_row = lax.broadcasted_iota(jnp.int32, (HEADS, DK), 0)
    head_lane = lax.broadcasted_iota(jnp.int32, (1, LANES), 1)
    n_out = [jnp.zeros((HEADS, DK), F32)] * n_sub
    m_out = [jnp.zeros((1, LANES), F32)] * n_sub

    for hd in range(HEADS):
        q = q_ref[:, hd * DK:(hd + 1) * DK] * BF16(DK ** -0.5)
        k = k_ref[:, hd * DK:(hd + 1) * DK]
        v = v_ref[:, hd * DV:(hd + 1) * DV]
        i_col = gates_c[:, hd:hd + 1]
        i_row = gates_r[hd:hd + 1, :]
        b_col = cum_c[:, HEADS + hd:HEADS + hd + 1]
        b_row = cum_r[HEADS + hd:HEADS + hd + 1, :]
        m_prev = m_prev_all[:, hd:hd + 1]
        b_last = b_last_all[:, HEADS + hd:HEADS + hd + 1]

        a = b_col + m_prev
        dm = jnp.where(causal, b_col - b_row + i_row, -jnp.inf)
        m_row = jnp.maximum(a, jnp.max(dm, axis=1, keepdims=True))
        w_intra = jnp.exp(dm - m_row)
        w_inter = jnp.exp(a - m_row)
        s = lax.dot_general(q, k, _NT, preferred_element_type=F32) * w_intra
        inter = jnp.concatenate(
            [jnp.dot(q_ref[sub * t_len:(sub + 1) * t_len, hd * DK:(hd + 1) * DK] * BF16(DK ** -0.5),
                     c0_ref[sub, hd].astype(BF16), preferred_element_type=F32)
             for sub in range(n_sub)], axis=0)
        num = w_inter * inter + jnp.dot(s.astype(BF16), v, preferred_element_type=F32)
        n_prev = per_seq(n0_ref[:, hd:hd + 1, :])
        den = (w_inter * jnp.sum(q.astype(F32) * n_prev, axis=1, keepdims=True)
               + jnp.sum(s, axis=1, keepdims=True))
        y_ref[:, hd * DV:(hd + 1) * DV] = _gated_heads(
            num, den, m_row, o_ref[:, hd * DV:(hd + 1) * DV], hn_ref[:, hd * DV:(hd + 1) * DV])

        m_new = last_token(m_row)
        decay = jnp.exp(b_last + m_prev - m_new)
        kw = k.astype(F32) * jnp.exp(b_last - b_col + i_col - m_new)
        for sub in range(n_sub):
            rows = slice(sub * t_len, (sub + 1) * t_len)
            last = slice((sub + 1) * t_len - 1, (sub + 1) * t_len)
            c_ref[sub, hd] = decay[last] * c0_ref[sub, hd] + lax.dot_general(
                kw[rows].astype(BF16), v_ref[rows, hd * DV:(hd + 1) * DV], _TN, preferred_element_type=F32)
            n_new = decay[last] * n0_ref[sub, hd:hd + 1, :] + jnp.sum(kw[rows], axis=0, keepdims=True)
            n_out[sub] = jnp.where(head_row == hd, n_new, n_out[sub])
            m_out[sub] = jnp.where(head_lane == hd, m_new[last], m_out[sub])

    for sub in range(n_sub):
        n_ref[sub] = n_out[sub]
        m_ref[sub] = m_out[sub]


def _mlstm_rec(j, proj, gates_c, head_norm, state, y, new_state, *, first_row, n_seq, t_len, chunk, n_sub):
    n_tok = proj.shape[0]
    n_chunks = t_len // chunk
    blk = n_sub * chunk
    base = first_row // blk
    rows = lambda width_blk: (lambda b, c: (base + b * n_chunks + c, width_blk))
    st_specs = [pl.BlockSpec((None, n_sub, HEADS, DK, DV), lambda b, c: (j, b, 0, 0, 0)),
                pl.BlockSpec((None, n_sub, HEADS, DK), lambda b, c: (j, b, 0, 0)),
                pl.BlockSpec((None, n_sub, 1, LANES), lambda b, c: (j, b, 0, 0))]
    carried = {}
    if y is not None:
        carried[0] = y
    if new_state is not None:
        carried.update({1: new_state[0], 2: new_state[1], 3: new_state[2]})
    if state is None:
        assert n_sub == 1
        body = functools.partial(_mlstm_prompt_kernel, chunk=chunk, n_carried=len(carried))
    else:
        assert n_chunks == 1
        body = functools.partial(_mlstm_sample_kernel, t_len=t_len, n_sub=n_sub, n_carried=len(carried))
    outs = _call(
        body, name="mlstm_rec_prompt" if state is None else "mlstm_rec_sample",
        grid=(n_seq // n_sub, n_chunks), sem=("arbitrary", "arbitrary"),
        in_specs=[pl.BlockSpec((blk, QK_DIM), rows(0)),
                  pl.BlockSpec((blk, QK_DIM), rows(1)),
                  pl.BlockSpec((blk, V_DIM), rows(1)),
                  pl.BlockSpec((blk, V_DIM), rows(2)),
                  pl.BlockSpec((blk, LANES), rows(0)),
                  _layer_row(j)] + ([] if state is None else st_specs),
        out_specs=[pl.BlockSpec((blk, V_DIM), rows(0))] + st_specs,
        out_shape=[jax.ShapeDtypeStruct((n_tok, V_DIM), BF16),
                   jax.ShapeDtypeStruct((N_MIX, n_seq, HEADS, DK, DV), F32),
                   jax.ShapeDtypeStruct((N_MIX, n_seq, HEADS, DK), F32),
                   jax.ShapeDtypeStruct((N_MIX, n_seq, 1, LANES), F32)],
        operands=(proj, proj, proj, proj, gates_c, head_norm) + (() if state is None else tuple(state)),
        carried=carried)
    return outs[0], tuple(outs[1:])


def _out_proj_kernel(x_ref, y_ref, w_ref, o_ref):
    o_ref[...] = x_ref[...] + jnp.dot(y_ref[...], w_ref[...], preferred_element_type=F32)


def _out_proj(j, h, y, w_bf):
    n_tok = h.shape[0]
    tm = 512
    row_spec = pl.BlockSpec((tm, D_MODEL), lambda i: (i, 0))
    return _call(
        _out_proj_kernel, name="mlstm_out", grid=(n_tok // tm,), sem=("parallel",),
        in_specs=[row_spec, pl.BlockSpec((tm, V_DIM), lambda i: (i, 0)),
                  pl.BlockSpec((None, V_DIM, D_MODEL), lambda i: (j, 0, 0))],
        out_specs=row_spec, out_shape=jax.ShapeDtypeStruct((n_tok, D_MODEL), F32),
        operands=(h, y, w_bf))


def _mlstm_layer(j, h, u, state_s, new_p, new_s, w_t, wg_t, b_gate, head_norm, w_out_bf, dims):
    batch, seq, dec_batch, t_len = dims
    proj, gates_c = _mlstm_in(j, u, w_t, wg_t, b_gate)
    y, new_p = _mlstm_rec(j, proj, gates_c, head_norm, None, None, new_p,
                          first_row=0, n_seq=batch, t_len=seq, chunk=256, n_sub=1)
    y, new_s = _mlstm_rec(j, proj, gates_c, head_norm, state_s, y, new_s,
                          first_row=batch * seq, n_seq=dec_batch, t_len=t_len, chunk=t_len, n_sub=4)
    return _out_proj(j, h, y, w_out_bf), new_p, new_s


def kernel(x_prompt, x_sample, state_pool, state_mlstm_C, state_mlstm_n, state_mlstm_m, norm_mix, norm_ffn, norm_final, pool_w, pool_scale, mlstm_w_in, mlstm_b_gate, mlstm_head_norm, mlstm_w_out, ffn_w_up, ffn_w_down):
    batch, seq, _ = x_prompt.shape
    dec_batch, t_len, _ = x_sample.shape
    dims = (batch, seq, dec_batch, t_len)
    n_prompt = batch * seq
    x_prompt = x_prompt.reshape(n_prompt, D_MODEL)
    x_sample = x_sample.reshape(dec_batch * t_len, D_MODEL)

    rows3 = lambda a: a.reshape(a.shape[0], 1, a.shape[1])
    norm_mix, norm_ffn, pool_scale, head_norm = map(rows3, (norm_mix, norm_ffn, pool_scale, mlstm_head_norm))
    pad_lanes = lambda a: jnp.pad(a, [(0, 0)] * (a.ndim - 1) + [(0, LANES - a.shape[-1])])
    b_gate = rows3(pad_lanes(mlstm_b_gate))
    state_s = (state_mlstm_C, state_mlstm_n, pad_lanes(state_mlstm_m)[:, :, None, :])
    pool_w_bf = pool_w.astype(BF16)
    w_in_t = jnp.swapaxes(mlstm_w_in, 1, 2)
    w_gate_t = jnp.pad(w_in_t[:, QKVO_DIM:, :], ((0, 0), (0, LANES - 2 * HEADS), (0, 0)))
    w_out_bf = mlstm_w_out.astype(BF16)
    ffn_bf = (ffn_w_up[0].astype(BF16), ffn_w_down[0].astype(BF16))
    norm_final = norm_final.reshape(1, D_MODEL)

    h = u = new_pool = new_p = new_s = None
    for layer in range(DEPTH):
        j = layer // 2
        if layer % 2 == 0:
            h, new_pool = _pool_layer(j, layer, h, x_prompt, x_sample, state_pool, new_pool,
                                      norm_mix, pool_w_bf, pool_scale, dims)
        else:
            h, new_p, new_s = _mlstm_layer(j, h, u, state_s, new_p, new_s, w_in_t, w_gate_t,
                                           b_gate, head_norm, w_out_bf, dims)
        h, u, ffn_bf = _ffn_layer(layer, h, norm_ffn, *ffn_bf, ffn_w_up, ffn_w_down, norm_mix, norm_final,
                                  n_prompt)

    y_prompt = h[0].reshape(batch, seq, D_MODEL)
    y_sample = h[1].reshape(dec_batch, t_len, D_MODEL)
    return (y_prompt, y_sample, new_pool[0], new_pool[1],
            new_p[0], new_p[1], new_p[2][:, :, 0, :HEADS],
            new_s[0], new_s[1], new_s[2][:, :, 0, :HEADS])
```

```python
import functools

import jax
import jax.numpy as jnp
from jax import lax
from jax.experimental import pallas as pl
from jax.experimental.pallas import tpu as pltpu

F32 = jnp.float32
BF16 = jnp.bfloat16

D_MODEL = 2048
DEPTH = 4
N_MIX = DEPTH // 2
POOL_WINDOWS = (2, 4, 8, 16)
N_GROUPS = len(POOL_WINDOWS)
GROUP_DIM = D_MODEL // N_GROUPS
POOL_BUF = max(POOL_WINDOWS) - 1
assert POOL_WINDOWS[0] == 2 and all(b == 2 * a for a, b in zip(POOL_WINDOWS, POOL_WINDOWS[1:]))
HALO = POOL_BUF + 1
PAD = 8
HEADS = 4
QK_DIM = D_MODEL // 2
V_DIM = D_MODEL
DK = QK_DIM // HEADS
DV = V_DIM // HEADS
QKVO_DIM = 2 * QK_DIM + 2 * V_DIM
GATE_CAP = 15.0
D_FF = 4 * D_MODEL
EPS = 1e-6
LANES = 128

VMEM_LIMIT = 60 * 1024 * 1024


def _rmsnorm(x, g):
    return x * lax.rsqrt(jnp.mean(x * x, axis=-1, keepdims=True) + EPS) * g


def _call(body, *, name, grid, sem, in_specs, out_specs, out_shape, operands, carried=None, aliases=None,
          scratch_shapes=()):
    carried = carried or {}
    aliases = dict(aliases or {})
    for pos, out_idx in enumerate(carried):
        aliases[len(operands) + pos] = out_idx
    return pl.pallas_call(
        body,
        grid=grid,
        in_specs=list(in_specs) + [pl.BlockSpec(memory_space=pl.ANY)] * len(carried),
        out_specs=out_specs,
        out_shape=out_shape,
        scratch_shapes=list(scratch_shapes),
        input_output_aliases=aliases,
        compiler_params=pltpu.CompilerParams(dimension_semantics=sem, vmem_limit_bytes=VMEM_LIMIT),
        name=name,
    )(*operands, *carried.values())


def _layer_row(layer, d=D_MODEL):
    return pl.BlockSpec((None, 1, d), lambda *_: (layer, 0, 0))


def _pool_prompt_kernel(h_ref, g_ref, w_ref, s_ref, *rest, tt, n_carried):
    o_ref, nb_ref, ext_ref, *lvl_refs = rest[n_carried:]
    t = pl.program_id(1)
    base = PAD + HALO
    end = base + tt

    @pl.when(t == 0)
    def _():
        ext_ref[0:base, :] = jnp.zeros((base, D_MODEL), F32)
        for ref in lvl_refs:
            ref[0:PAD, :] = jnp.zeros((PAD, D_MODEL), F32)

    x = h_ref[...]
    ext_ref[base:end, :] = _rmsnorm(x, g_ref[...])
    n_seen = t * tt + 1 + lax.broadcasted_iota(jnp.int32, (tt, 1), 0)
    src = ext_ref
    outs = []
    for g, w in enumerate(POOL_WINDOWS):
        half = w // 2
        c0 = g * GROUP_DIM
        cs = slice(c0, c0 + GROUP_DIM)
        if g < len(lvl_refs):
            dst = lvl_refs[g]
            dst[PAD:end, c0:] = src[PAD:end, c0:] + src[PAD - half:end - half, c0:]
            s = dst[base:end, cs]
            src = dst
        else:
            s = src[base:end, cs] + src[base - half:end - half, cs]
        cur = ext_ref[base:end, cs]
        d = s / jnp.minimum(n_seen, w).astype(F32) - cur
        outs.append(jnp.dot(d.astype(BF16), w_ref[g], preferred_element_type=F32))
    o_ref[...] = x + jnp.concatenate(outs, axis=-1) * s_ref[...]

    @pl.when(t == pl.num_programs(1) - 1)
    def _():
        nb_ref[0] = ext_ref[end - POOL_BUF:end, :]

    ext_ref[PAD:base, :] = ext_ref[end - HALO:end, :]


def _pool_sample_kernel(h_ref, buf_ref, g_ref, w_ref, s_ref, *rest, sb, t_len, n_carried):
    o_ref, nb_ref, ext_ref = rest[n_carried:]
    x = h_ref[...]
    u = _rmsnorm(x, g_ref[...])
    ext_ref[:, 1:HALO, :] = buf_ref[...]
    ext_ref[:, HALO:HALO + t_len, :] = u.reshape(sb, t_len, D_MODEL)
    outs = []
    for g, w in enumerate(POOL_WINDOWS):
        cs = slice(g * GROUP_DIM, (g + 1) * GROUP_DIM)
        cur = ext_ref[:, HALO:HALO + t_len, cs]
        s = cur
        for i in range(1, w):
            s = s + ext_ref[:, HALO - i:HALO - i + t_len, cs]
        d = (s * (1.0 / w) - cur).reshape(sb * t_len, GROUP_DIM)
        outs.append(jnp.dot(d.astype(BF16), w_ref[g], preferred_element_type=F32))
    o_ref[...] = x + jnp.concatenate(outs, axis=-1) * s_ref[...]
    nb_ref[...] = ext_ref[:, HALO + t_len - POOL_BUF:HALO + t_len, :]


def _pool_layer(j, layer, h, x_prompt, x_sample, state_pool, new_pool, gains, w_bf, scales, dims):
    batch, seq, dec_batch, t_len = dims
    n_prompt, n_tok = batch * seq, batch * seq + dec_batch * t_len
    first = h is None
    h_shape = jax.ShapeDtypeStruct((n_tok, D_MODEL), F32)
    w_spec = pl.BlockSpec((None, N_GROUPS, GROUP_DIM, GROUP_DIM), lambda *_: (j, 0, 0, 0))

    tt = 512
    nt = seq // tt
    row_spec = pl.BlockSpec((tt, D_MODEL), lambda b, t: (b * nt + t, 0))
    h, nb_p = _call(
        functools.partial(_pool_prompt_kernel, tt=tt, n_carried=0 if first else 1),
        name="pool_prompt", grid=(batch, nt), sem=("arbitrary", "arbitrary"),
        in_specs=[row_spec, _layer_row(layer), w_spec, _layer_row(j)],
        out_specs=[row_spec, pl.BlockSpec((None, 1, POOL_BUF, D_MODEL), lambda b, t: (j, b, 0, 0))],
        out_shape=[h_shape, jax.ShapeDtypeStruct((N_MIX, batch, POOL_BUF, D_MODEL), F32)],
        operands=(x_prompt if first else h, gains, w_bf, scales),
        carried=None if first else {1: new_pool[0]},
        aliases=None if first else {0: 0},
        scratch_shapes=[pltpu.VMEM((PAD + HALO + tt, D_MODEL), F32)] * N_GROUPS)

    sb = 16
    rows = sb * t_len
    base = n_prompt // rows
    out_rows = pl.BlockSpec((rows, D_MODEL), lambda i: (base + i, 0))
    h, nb_s = _call(
        functools.partial(_pool_sample_kernel, sb=sb, t_len=t_len, n_carried=1),
        name="pool_sample", grid=(dec_batch // sb,), sem=("arbitrary",),
        in_specs=[pl.BlockSpec((rows, D_MODEL), lambda i: (i, 0)) if first else out_rows,
                  pl.BlockSpec((None, sb, POOL_BUF, D_MODEL), lambda i: (j, i, 0, 0)),
                  _layer_row(layer), w_spec, _layer_row(j)],
        out_specs=[out_rows, pl.BlockSpec((None, sb, POOL_BUF, D_MODEL), lambda i: (j, i, 0, 0))],
        out_shape=[h_shape, jax.ShapeDtypeStruct((N_MIX, dec_batch, POOL_BUF, D_MODEL), F32)],
        operands=(x_sample if first else h, state_pool, gains, w_bf, scales),
        carried={0: h} if first else {1: new_pool[1]},
        aliases=None if first else {0: 0},
        scratch_shapes=[pltpu.VMEM((sb, HALO + t_len, D_MODEL), F32)])
    return h, (nb_p, nb_s)


def _ffn_tile(x_ref, g_ref, wu_ref, wd_ref, o_ref, u_ref):
    @pl.when(pl.program_id(1) == 0)
    def _():
        x = x_ref[...]
        u_ref[...] = _rmsnorm(x, g_ref[...]).astype(BF16)
        o_ref[...] = x

    a = jnp.maximum(jnp.dot(u_ref[...], wu_ref[...], preferred_element_type=F32), 0.0)
    o_ref[...] += jnp.dot((a * a).astype(BF16), wd_ref[...], preferred_element_type=F32)


def _ffn_kernel(x_ref, g_ref, wu_ref, wd_ref, *rest, emit_u, cast_next):
    rest = list(rest)
    gn_ref = rest.pop(0) if emit_u else None
    wun_ref, wdn_ref = (rest.pop(0), rest.pop(0)) if cast_next else (None, None)
    o_ref = rest.pop(0)
    un_ref = rest.pop(0) if emit_u else None
    if cast_next:
        wun_bf_ref, wdn_bf_ref = rest.pop(0), rest.pop(0)
        wun_bf_ref[...] = wun_ref[...].astype(BF16)
        wdn_bf_ref[...] = wdn_ref[...].astype(BF16)
    (u_ref,) = rest
    _ffn_tile(x_ref, g_ref, wu_ref, wd_ref, o_ref, u_ref)

    if emit_u:
        @pl.when(pl.program_id(1) == pl.num_programs(1) - 1)
        def _():
            un_ref[...] = _rmsnorm(o_ref[...], gn_ref[...]).astype(BF16)


def _ffn_final_kernel(x_ref, g_ref, wu_ref, wd_ref, gf_ref, yp_ref, ys_ref, u_ref, *, n_prompt_tiles):
    def tile(o_ref):
        _ffn_tile(x_ref, g_ref, wu_ref, wd_ref, o_ref, u_ref)

        @pl.when(pl.program_id(1) == pl.num_programs(1) - 1)
        def _():
            o_ref[...] = _rmsnorm(o_ref[...], gf_ref[...])

    is_prompt = pl.program_id(0) < n_prompt_tiles
    pl.when(is_prompt)(lambda: tile(yp_ref))
    pl.when(jnp.logical_not(is_prompt))(lambda: tile(ys_ref))


FFN_TM, FFN_FC = 512, 1024
FFN_FINAL_TILE = (512, 1024)
CAST_SPLIT = 16


def _ffn_layer(layer, h, gains, wu_bf, wd_bf, w_up_f32, w_down_f32, mix_gains, gain_final, n_prompt):
    n_tok = h.shape[0]
    tm, fc = FFN_FINAL_TILE if layer == DEPTH - 1 else (FFN_TM, FFN_FC)
    n_k = D_FF // fc
    row_spec = pl.BlockSpec((tm, D_MODEL), lambda i, k: (i, 0))
    in_specs = [row_spec, _layer_row(layer),
                pl.BlockSpec((D_MODEL, fc), lambda i, k: (0, k)),
                pl.BlockSpec((fc, D_MODEL), lambda i, k: (k, 0))]
    scratch = [pltpu.VMEM((tm, D_MODEL), BF16)]
    grid = (n_tok // tm, n_k)
    h_shape = jax.ShapeDtypeStruct((n_tok, D_MODEL), F32)
    if layer == DEPTH - 1:
        npt = n_prompt // tm
        ys = _call(
            functools.partial(_ffn_final_kernel, n_prompt_tiles=npt),
            name="ffn_final", grid=grid, sem=("arbitrary", "arbitrary"),
            in_specs=in_specs + [pl.BlockSpec((1, D_MODEL), lambda i, k: (0, 0))],
            out_specs=[pl.BlockSpec((tm, D_MODEL), lambda i, k: (jnp.minimum(i, npt - 1), 0)),
                       pl.BlockSpec((tm, D_MODEL), lambda i, k: (jnp.maximum(i - npt, 0), 0))],
            out_shape=[jax.ShapeDtypeStruct((n_prompt, D_MODEL), F32),
                       jax.ShapeDtypeStruct((n_tok - n_prompt, D_MODEL), F32)],
            operands=(h, gains, wu_bf, wd_bf, gain_final), scratch_shapes=scratch)
        return ys, None, None

    emit_u = layer % 2 == 0
    cast_blk = lambda i, k: (jnp.minimum(i, CAST_SPLIT - 1), jnp.where(i < CAST_SPLIT, k, n_k - 1))
    up_blk, down_blk = (D_MODEL // CAST_SPLIT, fc), (D_FF // CAST_SPLIT, D_MODEL // n_k)
    operands, out_specs, out_shape = [h, gains, wu_bf, wd_bf], [row_spec], [h_shape]
    if emit_u:
        in_specs.append(_layer_row(layer + 1))
        operands.append(mix_gains)
        out_specs.append(row_spec)
        out_shape.append(jax.ShapeDtypeStruct((n_tok, D_MODEL), BF16))
    in_specs += [pl.BlockSpec((None,) + up_blk, lambda i, k: (layer + 1,) + cast_blk(i, k)),
                 pl.BlockSpec((None,) + down_blk, lambda i, k: (layer + 1,) + cast_blk(i, k))]
    operands += [w_up_f32, w_down_f32]
    out_specs += [pl.BlockSpec(up_blk, cast_blk), pl.BlockSpec(down_blk, cast_blk)]
    out_shape += [jax.ShapeDtypeStruct((D_MODEL, D_FF), BF16), jax.ShapeDtypeStruct((D_FF, D_MODEL), BF16)]
    outs = _call(functools.partial(_ffn_kernel, emit_u=emit_u, cast_next=True),
                 name="ffn", grid=grid, sem=("arbitrary", "arbitrary"),
                 in_specs=in_specs, out_specs=out_specs, out_shape=out_shape,
                 operands=tuple(operands), scratch_shapes=scratch)
    return outs[0], (outs[1] if emit_u else None), tuple(outs[-2:])


_NT = (((1,), (1,)), ((), ()))


def _mlstm_in_kernel(u_ref, w_ref, wg_ref, bg_ref, p_ref, gc_ref, w_scr):
    jj, i = pl.program_id(0), pl.program_id(1)

    @pl.when(i == 0)
    def _():
        w_scr[...] = w_ref[...].astype(BF16)

    u = u_ref[...]
    p_ref[...] = lax.dot_general(u, w_scr[...], _NT, preferred_element_type=F32).astype(BF16)

    @pl.when(jj == 0)
    def _():
        pre = lax.dot_general(u, wg_ref[...].astype(BF16), _NT, preferred_element_type=F32) + bg_ref[...]
        capped = GATE_CAP * jnp.tanh(pre / GATE_CAP)
        log_f = jnp.minimum(capped, 0.0) - jnp.log1p(jnp.exp(-jnp.abs(capped)))
        lane = lax.broadcasted_iota(jnp.int32, capped.shape, 1)
        gc_ref[...] = jnp.where(lane < HEADS, capped, log_f)


def _mlstm_in(j, u, w_t, wg_t, b_gate):
    n_tok = u.shape[0]
    tm, nb = 1024, 1024
    n_i = n_tok // tm
    return _call(
        _mlstm_in_kernel, name="mlstm_in", grid=(QKVO_DIM // nb, n_i), sem=("arbitrary", "arbitrary"),
        in_specs=[pl.BlockSpec((tm, D_MODEL), lambda jj, i: (i, 0)),
                  pl.BlockSpec((None, nb, D_MODEL), lambda jj, i: (j, jj, 0)),
                  pl.BlockSpec((None, LANES, D_MODEL), lambda jj, i: (j, 0, 0)),
                  _layer_row(j, LANES)],
        out_specs=[pl.BlockSpec((tm, nb), lambda jj, i: (i, jj)),
                   pl.BlockSpec((tm, LANES), lambda jj, i: (jnp.where(jj == 0, i, n_i - 1), 0))],
        out_shape=[jax.ShapeDtypeStruct((n_tok, QKVO_DIM), BF16),
                   jax.ShapeDtypeStruct((n_tok, LANES), F32)],
        operands=(u, w_t, wg_t, b_gate),
        scratch_shapes=[pltpu.VMEM((nb, D_MODEL), BF16)])


def _dot_f32(lhs, rhs, dims):
    return lax.dot_general(lhs, rhs, (dims, ((), ())), preferred_element_type=F32,
                           precision=lax.Precision.HIGHEST)


_TN = (((0,), (0,)), ((), ()))


def _gated_heads(num, den, m_row, o, hn):
    hh = num / jnp.maximum(jnp.abs(den), jnp.exp(-m_row))
    hh = hh * lax.rsqrt(jnp.mean(hh * hh, axis=-1, keepdims=True) + EPS)
    hh = hh * hn
    return (jax.nn.sigmoid(o.astype(F32)) * hh).astype(BF16)


def _gate_sums(gates_c, causal, causal_t):
    pick = (lax.broadcasted_iota(jnp.int32, (2 * HEADS, LANES), 0)
            == lax.broadcasted_iota(jnp.int32, (2 * HEADS, LANES), 1)).astype(F32)
    gates_r = _dot_f32(pick, gates_c, ((1,), (1,)))
    cum_c = _dot_f32(causal.astype(F32), gates_c, ((1,), (0,)))
    cum_r = _dot_f32(gates_r, causal_t.astype(F32), ((1,), (0,)))
    return gates_r, cum_c, cum_r


def _mlstm_prompt_chunk(q_ref, k_ref, v_ref, o_ref, gc_ref, hn_ref, y_ref, c_ref, n_ref, m_ref, *, chunk):
    @pl.when(pl.program_id(1) == 0)
    def _():
        c_ref[...] = jnp.zeros(c_ref.shape, F32)
        n_ref[...] = jnp.zeros(n_ref.shape, F32)
        m_ref[...] = jnp.zeros(m_ref.shape, F32)

    head_row = lax.broadcasted_iota(jnp.int32, (HEADS, DK), 0)
    head_lane = lax.broadcasted_iota(jnp.int32, (1, LANES), 1)
    row = lax.broadcasted_iota(jnp.int32, (chunk, chunk), 0)
    col = lax.broadcasted_iota(jnp.int32, (chunk, chunk), 1)
    causal = col <= row
    gates_c = gc_ref[...]
    gates_r, cum_c, cum_r = _gate_sums(gates_c, causal, row <= col)
    n_all = n_ref[0]
    m_all = m_ref[0]
    n_out = jnp.zeros((HEADS, DK), F32)
    m_out = jnp.zeros((1, LANES), F32)

    for hd in range(HEADS):
        q = q_ref[:, hd * DK:(hd + 1) * DK] * BF16(DK ** -0.5)
        k = k_ref[:, hd * DK:(hd + 1) * DK]
        v = v_ref[:, hd * DV:(hd + 1) * DV]
        i_col = gates_c[:, hd:hd + 1]
        i_row = gates_r[hd:hd + 1, :]
        b_col = cum_c[:, HEADS + hd:HEADS + hd + 1]
        b_row = cum_r[HEADS + hd:HEADS + hd + 1, :]
        c_prev = c_ref[0, hd]
        m_prev = m_all[:, hd:hd + 1]

        a = b_col + m_prev
        dm = jnp.where(causal, b_col - b_row + i_row, -jnp.inf)
        m_row = jnp.maximum(a, jnp.max(dm, axis=1, keepdims=True))
        w_intra = jnp.exp(dm - m_row)
        w_inter = jnp.exp(a - m_row)
        s = lax.dot_general(q, k, _NT, preferred_element_type=F32) * w_intra
        num = (w_inter * jnp.dot(q, c_prev.astype(BF16), preferred_element_type=F32)
               + jnp.dot(s.astype(BF16), v, preferred_element_type=F32))
        den = (w_inter * jnp.sum(q.astype(F32) * n_all[hd:hd + 1, :], axis=1, keepdims=True)
               + jnp.sum(s, axis=1, keepdims=True))

        b_last = b_col[chunk - 1:chunk, :]
        m_new = m_row[chunk - 1:chunk, :]
        decay = jnp.exp(b_last + m_prev - m_new)
        kw = k.astype(F32) * jnp.exp(b_last - b_col + i_col - m_new)
        c_ref[0, hd] = decay * c_prev + lax.dot_general(kw.astype(BF16), v, _TN, preferred_element_type=F32)
        n_new = decay * n_all[hd:hd + 1, :] + jnp.sum(kw, axis=0, keepdims=True)
        n_out = jnp.where(head_row == hd, n_new, n_out)
        m_out = jnp.where(head_lane == hd, m_new, m_out)

        y_ref[:, hd * DV:(hd + 1) * DV] = _gated_heads(
            num, den, m_row, o_ref[:, hd * DV:(hd + 1) * DV], hn_ref[:, hd * DV:(hd + 1) * DV])

    n_ref[0] = n_out
    m_ref[0] = m_out


def _mlstm_sample_group(q_ref, k_ref, v_ref, o_ref, gc_ref, hn_ref, c0_ref, n0_ref, m0_ref,
                        y_ref, c_ref, n_ref, m_ref, *, t_len, n_sub):
    n_rows = n_sub * t_len
    row = lax.broadcasted_iota(jnp.int32, (n_rows, n_rows), 0)
    col = lax.broadcasted_iota(jnp.int32, (n_rows, n_rows), 1)
    seq_start = lax.broadcasted_iota(jnp.int32, (n_sub, t_len, n_rows), 0).reshape(n_rows, n_rows) * t_len
    same_seq = (col >= seq_start) & (col < seq_start + t_len)
    causal = same_seq & (col <= row)
    causal_t = same_seq & (row <= col)

    def per_seq(x):
        return jnp.broadcast_to(x, (n_sub, t_len, x.shape[-1])).reshape(n_rows, x.shape[-1])

    def last_token(x):
        return per_seq(x.reshape(n_sub, t_len, x.shape[-1])[:, t_len - 1:t_len, :])

    gates_c = gc_ref[...]
    gates_r, cum_c, cum_r = _gate_sums(gates_c, causal, causal_t)
    m_prev_all = per_seq(m0_ref[...])
    b_last_all = last_token(cum_c)
    head_row = lax.broadcasted_iota(jnp.int32, (HEADS, DK), 0)
    head_lane = lax.broadcasted_iota(jnp.int32, (1, LANES), 1)
    n_out = [jnp.zeros((HEADS, DK), F32)] * n_sub
    m_out = [jnp.zeros((1, LANES), F32)] * n_sub

    for hd in range(HEADS):
        q = q_ref[:, hd * DK:(hd + 1) * DK] * BF16(DK ** -0.5)
        k = k_ref[:, hd * DK:(hd + 1) * DK]
        v = v_ref[:, hd * DV:(hd + 1) * DV]
        i_col = gates_c[:, hd:hd + 1]
        i_row = gates_r[hd:hd + 1, :]
        b_col = cum_c[:, HEADS + hd:HEADS + hd + 1]
        b_row = cum_r[HEADS + hd:HEADS + hd + 1, :]
        m_prev = m_prev_all[:, hd:hd + 1]
        b_last = b_last_all[:, HEADS + hd:HEADS + hd + 1]

        a = b_col + m_prev
        dm = jnp.where(causal, b_col - b_row + i_row, -jnp.inf)
        m_row = jnp.maximum(a, jnp.max(dm, axis=1, keepdims=True))
        w_intra = jnp.exp(dm - m_row)
        w_inter = jnp.exp(a - m_row)
        s = lax.dot_general(q, k, _NT, preferred_element_type=F32) * w_intra
        inter = jnp.concatenate(
            [jnp.dot(q_ref[sub * t_len:(sub + 1) * t_len, hd * DK:(hd + 1) * DK] * BF16(DK ** -0.5),
                     c0_ref[sub, hd].astype(BF16), preferred_element_type=F32)
             for sub in range(n_sub)], axis=0)
        num = w_inter * inter + jnp.dot(s.astype(BF16), v, preferred_element_type=F32)
        n_prev = per_seq(n0_ref[:, hd:hd + 1, :])
        den = (w_inter * jnp.sum(q.astype(F32) * n_prev, axis=1, keepdims=True)
               + jnp.sum(s, axis=1, keepdims=True))
        y_ref[:, hd * DV:(hd + 1) * DV] = _gated_heads(
            num, den, m_row, o_ref[:, hd * DV:(hd + 1) * DV], hn_ref[:, hd * DV:(hd + 1) * DV])

        m_new = last_token(m_row)
        decay = jnp.exp(b_last + m_prev - m_new)
        kw = k.astype(F32) * jnp.exp(b_last - b_col + i_col - m_new)
        for sub in range(n_sub):
            rows = slice(sub * t_len, (sub + 1) * t_len)
            last = slice((sub + 1) * t_len - 1, (sub + 1) * t_len)
            c_ref[sub, hd] = decay[last] * c0_ref[sub, hd] + lax.dot_general(
                kw[rows].astype(BF16), v_ref[rows, hd * DV:(hd + 1) * DV], _TN, preferred_element_type=F32)
            n_new = decay[last] * n0_ref[sub, hd:hd + 1, :] + jnp.sum(kw[rows], axis=0, keepdims=True)
            n_out[sub] = jnp.where(head_row == hd, n_new, n_out[sub])
            m_out[sub] = jnp.where(head_lane == hd, m_new[last], m_out[sub])

    for sub in range(n_sub):
        n_ref[sub] = n_out[sub]
        m_ref[sub] = m_out[sub]


N_PROJ_REFS = 5
REC_CHUNK = 256
REC_GROUP = 4


def _mlstm_rec_kernel(*refs, chunk, t_len, n_sub, n_carried):
    p_in, s_in = refs[:N_PROJ_REFS], refs[N_PROJ_REFS:2 * N_PROJ_REFS]
    hn_ref, c0_ref, n0_ref, m0_ref = refs[2 * N_PROJ_REFS:2 * N_PROJ_REFS + 4]
    outs = refs[2 * N_PROJ_REFS + 4 + n_carried:]
    yp_ref, ys_ref = outs[:2]
    _mlstm_prompt_chunk(*p_in, hn_ref, yp_ref, *outs[2:5], chunk=chunk)
    _mlstm_sample_group(*s_in, hn_ref, c0_ref, n0_ref, m0_ref, ys_ref, *outs[5:8], t_len=t_len, n_sub=n_sub)


def _mlstm_rec(j, proj, gates_c, head_norm, state_s, new_p, new_s, dims):
    batch, seq, dec_batch, t_len = dims
    n_tok = proj.shape[0]
    chunk, n_sub = REC_CHUNK, REC_GROUP
    n_chunks = seq // chunk
    assert batch * n_chunks == dec_batch // n_sub
    s_blk = n_sub * t_len
    s_base = batch * seq // s_blk

    def proj_specs(blk, base):
        rows = lambda width_blk: (lambda b, c: (base + b * n_chunks + c, width_blk))
        return [pl.BlockSpec((blk, QK_DIM), rows(0)), pl.BlockSpec((blk, QK_DIM), rows(1)),
                pl.BlockSpec((blk, V_DIM), rows(1)), pl.BlockSpec((blk, V_DIM), rows(2)),
                pl.BlockSpec((blk, LANES), rows(0))]

    def state_specs(n_seq_blk, seq_blk_of):
        return [pl.BlockSpec((None, n_seq_blk, HEADS, DK, DV), lambda b, c: (j, seq_blk_of(b, c), 0, 0, 0)),
                pl.BlockSpec((None, n_seq_blk, HEADS, DK), lambda b, c: (j, seq_blk_of(b, c), 0, 0)),
                pl.BlockSpec((None, n_seq_blk, 1, LANES), lambda b, c: (j, seq_blk_of(b, c), 0, 0))]

    def state_shapes(n_seq):
        return [jax.ShapeDtypeStruct((N_MIX, n_seq, HEADS, DK, DV), F32),
                jax.ShapeDtypeStruct((N_MIX, n_seq, HEADS, DK), F32),
                jax.ShapeDtypeStruct((N_MIX, n_seq, 1, LANES), F32)]

    p_state = state_specs(1, lambda b, c: b)
    s_state = state_specs(n_sub, lambda b, c: b * n_chunks + c)
    carried = {}
    if new_p is not None:
        carried.update({2 + i: a for i, a in enumerate(new_p)})
        carried.update({5 + i: a for i, a in enumerate(new_s)})
    proj_ops = (proj, proj, proj, proj, gates_c)
    outs = _call(
        functools.partial(_mlstm_rec_kernel, chunk=chunk, t_len=t_len, n_sub=n_sub, n_carried=len(carried)),
        name="mlstm_rec", grid=(batch, n_chunks), sem=("arbitrary", "arbitrary"),
        in_specs=proj_specs(chunk, 0) + proj_specs(s_blk, s_base) + [_layer_row(j)] + s_state,
        out_specs=[pl.BlockSpec((chunk, V_DIM), lambda b, c: (b * n_chunks + c, 0)),
                   pl.BlockSpec((s_blk, V_DIM), lambda b, c: (b * n_chunks + c, 0))] + p_state + s_state,
        out_shape=[jax.ShapeDtypeStruct((batch * seq, V_DIM), BF16),
                   jax.ShapeDtypeStruct((dec_batch * t_len, V_DIM), BF16)]
                  + state_shapes(batch) + state_shapes(dec_batch),
        operands=proj_ops + proj_ops + (head_norm,) + tuple(state_s),
        carried=carried)
    return outs[0], outs[1], tuple(outs[2:5]), tuple(outs[5:8])


def _out_proj_kernel(x_ref, yp_ref, ys_ref, w_ref, o_ref, *, n_prompt_tiles):
    y = jnp.where(pl.program_id(0) < n_prompt_tiles, yp_ref[...], ys_ref[...])
    o_ref[...] = x_ref[...] + jnp.dot(y, w_ref[...], preferred_element_type=F32)


def _out_proj(j, h, y_prompt, y_sample, w_bf, n_prompt):
    n_tok = h.shape[0]
    tm = 512
    npt = n_prompt // tm
    row_spec = pl.BlockSpec((tm, D_MODEL), lambda i: (i, 0))
    return _call(
        functools.partial(_out_proj_kernel, n_prompt_tiles=npt),
        name="mlstm_out", grid=(n_tok // tm,), sem=("arbitrary",),
        in_specs=[row_spec,
                  pl.BlockSpec((tm, V_DIM), lambda i: (jnp.minimum(i, npt - 1), 0)),
                  pl.BlockSpec((tm, V_DIM), lambda i: (jnp.maximum(i - npt, 0), 0)),
                  pl.BlockSpec((None, V_DIM, D_MODEL), lambda i: (j, 0, 0))],
        out_specs=row_spec, out_shape=jax.ShapeDtypeStruct((n_tok, D_MODEL), F32),
        operands=(h, y_prompt, y_sample, w_bf))


def _mlstm_layer(j, h, u, state_s, new_p, new_s, w_t, wg_t, b_gate, head_norm, w_out_bf, dims):
    proj, gates_c = _mlstm_in(j, u, w_t, wg_t, b_gate)
    y_prompt, y_sample, new_p, new_s = _mlstm_rec(j, proj, gates_c, head_norm, state_s, new_p, new_s, dims)
    return _out_proj(j, h, y_prompt, y_sample, w_out_bf, dims[0] * dims[1]), new_p, new_s


def kernel(x_prompt, x_sample, state_pool, state_mlstm_C, state_mlstm_n, state_mlstm_m, norm_mix, norm_ffn, norm_final, pool_w, pool_scale, mlstm_w_in, mlstm_b_gate, mlstm_head_norm, mlstm_w_out, ffn_w_up, ffn_w_down):
    batch, seq, _ = x_prompt.shape
    dec_batch, t_len, _ = x_sample.shape
    dims = (batch, seq, dec_batch, t_len)
    n_prompt = batch * seq
    x_prompt = x_prompt.reshape(n_prompt, D_MODEL)
    x_sample = x_sample.reshape(dec_batch * t_len, D_MODEL)

    rows3 = lambda a: a.reshape(a.shape[0], 1, a.shape[1])
    norm_mix, norm_ffn, pool_scale, head_norm = map(rows3, (norm_mix, norm_ffn, pool_scale, mlstm_head_norm))
    pad_lanes = lambda a: jnp.pad(a, [(0, 0)] * (a.ndim - 1) + [(0, LANES - a.shape[-1])])
    b_gate = rows3(pad_lanes(mlstm_b_gate))
    state_s = (state_mlstm_C, state_mlstm_n, pad_lanes(state_mlstm_m)[:, :, None, :])
    pool_w_bf = pool_w.astype(BF16)
    w_in_t = jnp.swapaxes(mlstm_w_in, 1, 2)
    w_gate_t = jnp.pad(w_in_t[:, QKVO_DIM:, :], ((0, 0), (0, LANES - 2 * HEADS), (0, 0)))
    w_out_bf = mlstm_w_out.astype(BF16)
    ffn_bf = (ffn_w_up[0].astype(BF16), ffn_w_down[0].astype(BF16))
    norm_final = norm_final.reshape(1, D_MODEL)

    h = u = new_pool = new_p = new_s = None
    for layer in range(DEPTH):
        j = layer // 2
        if layer % 2 == 0:
            h, new_pool = _pool_layer(j, layer, h, x_prompt, x_sample, state_pool, new_pool,
                                      norm_mix, pool_w_bf, pool_scale, dims)
        else:
            h, new_p, new_s = _mlstm_layer(j, h, u, state_s, new_p, new_s, w_in_t, w_gate_t,
                                           b_gate, head_norm, w_out_bf, dims)
        h, u, ffn_bf = _ffn_layer(layer, h, norm_ffn, *ffn_bf, ffn_w_up, ffn_w_down, norm_mix, norm_final,
                                  n_prompt)

    y_prompt = h[0].reshape(batch, seq, D_MODEL)
    y_sample = h[1].reshape(dec_batch, t_len, D_MODEL)
    return (y_prompt, y_sample, new_pool[0], new_pool[1],
            new_p[0], new_p[1], new_p[2][:, :, 0, :HEADS],
            new_s[0], new_s[1], new_s[2][:, :, 0, :HEADS])
```

```python
import functools

import jax
import jax.numpy as jnp
from jax import lax
from jax.experimental import pallas as pl
from jax.experimental.pallas import tpu as pltpu

F32 = jnp.float32
BF16 = jnp.bfloat16

D_MODEL = 2048
DEPTH = 4
N_MIX = DEPTH // 2
POOL_WINDOWS = (2, 4, 8, 16)
N_GROUPS = len(POOL_WINDOWS)
GROUP_DIM = D_MODEL // N_GROUPS
POOL_BUF = max(POOL_WINDOWS) - 1
assert POOL_WINDOWS[0] == 2 and all(b == 2 * a for a, b in zip(POOL_WINDOWS, POOL_WINDOWS[1:]))
HALO = POOL_BUF + 1
PAD = 8
HEADS = 4
QK_DIM = D_MODEL // 2
V_DIM = D_MODEL
DK = QK_DIM // HEADS
DV = V_DIM // HEADS
QKVO_DIM = 2 * QK_DIM + 2 * V_DIM
GATE_CAP = 15.0
D_FF = 4 * D_MODEL
EPS = 1e-6
LANES = 128

VMEM_LIMIT = 60 * 1024 * 1024


def _rmsnorm(x, g):
    return x * lax.rsqrt(jnp.mean(x * x, axis=-1, keepdims=True) + EPS) * g


def _call(body, *, name, grid, sem, in_specs, out_specs, out_shape, operands, carried=None, aliases=None,
          scratch_shapes=()):
    carried = carried or {}
    aliases = dict(aliases or {})
    for pos, out_idx in enumerate(carried):
        aliases[len(operands) + pos] = out_idx
    return pl.pallas_call(
        body,
        grid=grid,
        in_specs=list(in_specs) + [pl.BlockSpec(memory_space=pl.ANY)] * len(carried),
        out_specs=out_specs,
        out_shape=out_shape,
        scratch_shapes=list(scratch_shapes),
        input_output_aliases=aliases,
        compiler_params=pltpu.CompilerParams(dimension_semantics=sem, vmem_limit_bytes=VMEM_LIMIT),
        name=name,
    )(*operands, *carried.values())


def _layer_row(layer, d=D_MODEL):
    return pl.BlockSpec((None, 1, d), lambda *_: (layer, 0, 0))


def _pool_prompt_kernel(h_ref, g_ref, w_ref, s_ref, *rest, tt, n_carried):
    o_ref, nb_ref, ext_ref, *lvl_refs = rest[n_carried:]
    t = pl.program_id(1)
    base = PAD + HALO
    end = base + tt

    @pl.when(t == 0)
    def _():
        ext_ref[0:base, :] = jnp.zeros((base, D_MODEL), F32)
        for ref in lvl_refs:
            ref[0:PAD, :] = jnp.zeros((PAD, D_MODEL), F32)

    x = h_ref[...]
    ext_ref[base:end, :] = _rmsnorm(x, g_ref[...])
    n_seen = t * tt + 1 + lax.broadcasted_iota(jnp.int32, (tt, 1), 0)
    src = ext_ref
    outs = []
    for g, w in enumerate(POOL_WINDOWS):
        half = w // 2
        c0 = g * GROUP_DIM
        cs = slice(c0, c0 + GROUP_DIM)
        if g < len(lvl_refs):
            dst = lvl_refs[g]
            dst[PAD:end, c0:] = src[PAD:end, c0:] + src[PAD - half:end - half, c0:]
            s = dst[base:end, cs]
            src = dst
        else:
            s = src[base:end, cs] + src[base - half:end - half, cs]
        cur = ext_ref[base:end, cs]
        d = s / jnp.minimum(n_seen, w).astype(F32) - cur
        outs.append(jnp.dot(d.astype(BF16), w_ref[g], preferred_element_type=F32))
    o_ref[...] = x + jnp.concatenate(outs, axis=-1) * s_ref[...]

    @pl.when(t == pl.num_programs(1) - 1)
    def _():
        nb_ref[0] = ext_ref[end - POOL_BUF:end, :]

    ext_ref[PAD:base, :] = ext_ref[end - HALO:end, :]


def _pool_sample_kernel(h_ref, buf_ref, g_ref, w_ref, s_ref, *rest, sb, t_len, n_carried):
    o_ref, nb_ref, ext_ref = rest[n_carried:]
    x = h_ref[...]
    u = _rmsnorm(x, g_ref[...])
    ext_ref[:, 1:HALO, :] = buf_ref[...]
    ext_ref[:, HALO:HALO + t_len, :] = u.reshape(sb, t_len, D_MODEL)
    outs = []
    for g, w in enumerate(POOL_WINDOWS):
        cs = slice(g * GROUP_DIM, (g + 1) * GROUP_DIM)
        cur = ext_ref[:, HALO:HALO + t_len, cs]
        s = cur
        for i in range(1, w):
            s = s + ext_ref[:, HALO - i:HALO - i + t_len, cs]
        d = (s * (1.0 / w) - cur).reshape(sb * t_len, GROUP_DIM)
        outs.append(jnp.dot(d.astype(BF16), w_ref[g], preferred_element_type=F32))
    o_ref[...] = x + jnp.concatenate(outs, axis=-1) * s_ref[...]
    nb_ref[...] = ext_ref[:, HALO + t_len - POOL_BUF:HALO + t_len, :]


def _pool_layer(j, layer, h, x_prompt, x_sample, state_pool, new_pool, gains, w_bf, scales, dims):
    batch, seq, dec_batch, t_len = dims
    n_prompt, n_tok = batch * seq, batch * seq + dec_batch * t_len
    first = h is None
    h_shape = jax.ShapeDtypeStruct((n_tok, D_MODEL), F32)
    w_spec = pl.BlockSpec((None, N_GROUPS, GROUP_DIM, GROUP_DIM), lambda *_: (j, 0, 0, 0))

    tt = 512
    nt = seq // tt
    row_spec = pl.BlockSpec((tt, D_MODEL), lambda b, t: (b * nt + t, 0))
    h, nb_p = _call(
        functools.partial(_pool_prompt_kernel, tt=tt, n_carried=0 if first else 1),
        name="pool_prompt", grid=(batch, nt), sem=("arbitrary", "arbitrary"),
        in_specs=[row_spec, _layer_row(layer), w_spec, _layer_row(j)],
        out_specs=[row_spec, pl.BlockSpec((None, 1, POOL_BUF, D_MODEL), lambda b, t: (j, b, 0, 0))],
        out_shape=[h_shape, jax.ShapeDtypeStruct((N_MIX, batch, POOL_BUF, D_MODEL), F32)],
        operands=(x_prompt if first else h, gains, w_bf, scales),
        carried=None if first else {1: new_pool[0]},
        aliases=None if first else {0: 0},
        scratch_shapes=[pltpu.VMEM((PAD + HALO + tt, D_MODEL), F32)] * N_GROUPS)

    sb = 16
    rows = sb * t_len
    base = n_prompt // rows
    out_rows = pl.BlockSpec((rows, D_MODEL), lambda i: (base + i, 0))
    h, nb_s = _call(
        functools.partial(_pool_sample_kernel, sb=sb, t_len=t_len, n_carried=1),
        name="pool_sample", grid=(dec_batch // sb,), sem=("arbitrary",),
        in_specs=[pl.BlockSpec((rows, D_MODEL), lambda i: (i, 0)) if first else out_rows,
                  pl.BlockSpec((None, sb, POOL_BUF, D_MODEL), lambda i: (j, i, 0, 0)),
                  _layer_row(layer), w_spec, _layer_row(j)],
        out_specs=[out_rows, pl.BlockSpec((None, sb, POOL_BUF, D_MODEL), lambda i: (j, i, 0, 0))],
        out_shape=[h_shape, jax.ShapeDtypeStruct((N_MIX, dec_batch, POOL_BUF, D_MODEL), F32)],
        operands=(x_sample if first else h, state_pool, gains, w_bf, scales),
        carried={0: h} if first else {1: new_pool[1]},
        aliases=None if first else {0: 0},
        scratch_shapes=[pltpu.VMEM((sb, HALO + t_len, D_MODEL), F32)])
    return h, (nb_p, nb_s)


def _ffn_tile(x_ref, g_ref, wu_ref, wd_ref, o_ref, u_ref):
    @pl.when(pl.program_id(1) == 0)
    def _():
        x = x_ref[...]
        u_ref[...] = _rmsnorm(x, g_ref[...]).astype(BF16)
        o_ref[...] = x

    a = jnp.maximum(jnp.dot(u_ref[...], wu_ref[...], preferred_element_type=F32), 0.0)
    o_ref[...] += jnp.dot((a * a).astype(BF16), wd_ref[...], preferred_element_type=F32)


def _ffn_kernel(x_ref, g_ref, wu_ref, wd_ref, *rest, emit_u, cast_next):
    rest = list(rest)
    gn_ref = rest.pop(0) if emit_u else None
    wun_ref, wdn_ref = (rest.pop(0), rest.pop(0)) if cast_next else (None, None)
    o_ref = rest.pop(0)
    un_ref = rest.pop(0) if emit_u else None
    if cast_next:
        wun_bf_ref, wdn_bf_ref = rest.pop(0), rest.pop(0)
        wun_bf_ref[...] = wun_ref[...].astype(BF16)
        wdn_bf_ref[...] = wdn_ref[...].astype(BF16)
    (u_ref,) = rest
    _ffn_tile(x_ref, g_ref, wu_ref, wd_ref, o_ref, u_ref)

    if emit_u:
        @pl.when(pl.program_id(1) == pl.num_programs(1) - 1)
        def _():
            un_ref[...] = _rmsnorm(o_ref[...], gn_ref[...]).astype(BF16)


def _ffn_final_kernel(x_ref, g_ref, wu_ref, wd_ref, gf_ref, yp_ref, ys_ref, u_ref, *, n_prompt_tiles):
    def tile(o_ref):
        _ffn_tile(x_ref, g_ref, wu_ref, wd_ref, o_ref, u_ref)

        @pl.when(pl.program_id(1) == pl.num_programs(1) - 1)
        def _():
            o_ref[...] = _rmsnorm(o_ref[...], gf_ref[...])

    is_prompt = pl.program_id(0) < n_prompt_tiles
    pl.when(is_prompt)(lambda: tile(yp_ref))
    pl.when(jnp.logical_not(is_prompt))(lambda: tile(ys_ref))


FFN_TM, FFN_FC = 512, 1024
FFN_FINAL_TILE = (512, 1024)
CAST_SPLIT = 16


def _ffn_layer(layer, h, gains, wu_bf, wd_bf, w_up_f32, w_down_f32, mix_gains, gain_final, n_prompt):
    n_tok = h.shape[0]
    tm, fc = FFN_FINAL_TILE if layer == DEPTH - 1 else (FFN_TM, FFN_FC)
    n_k = D_FF // fc
    row_spec = pl.BlockSpec((tm, D_MODEL), lambda i, k: (i, 0))
    in_specs = [row_spec, _layer_row(layer),
                pl.BlockSpec((D_MODEL, fc), lambda i, k: (0, k)),
                pl.BlockSpec((fc, D_MODEL), lambda i, k: (k, 0))]
    scratch = [pltpu.VMEM((tm, D_MODEL), BF16)]
    grid = (n_tok // tm, n_k)
    h_shape = jax.ShapeDtypeStruct((n_tok, D_MODEL), F32)
    if layer == DEPTH - 1:
        npt = n_prompt // tm
        ys = _call(
            functools.partial(_ffn_final_kernel, n_prompt_tiles=npt),
            name="ffn_final", grid=grid, sem=("arbitrary", "arbitrary"),
            in_specs=in_specs + [pl.BlockSpec((1, D_MODEL), lambda i, k: (0, 0))],
            out_specs=[pl.BlockSpec((tm, D_MODEL), lambda i, k: (jnp.minimum(i, npt - 1), 0)),
                       pl.BlockSpec((tm, D_MODEL), lambda i, k: (jnp.maximum(i - npt, 0), 0))],
            out_shape=[jax.ShapeDtypeStruct((n_prompt, D_MODEL), F32),
                       jax.ShapeDtypeStruct((n_tok - n_prompt, D_MODEL), F32)],
            operands=(h, gains, wu_bf, wd_bf, gain_final), scratch_shapes=scratch)
        return ys, None, None

    emit_u = layer % 2 == 0
    cast_blk = lambda i, k: (jnp.minimum(i, CAST_SPLIT - 1), jnp.where(i < CAST_SPLIT, k, n_k - 1))
    up_blk, down_blk = (D_MODEL // CAST_SPLIT, fc), (D_FF // CAST_SPLIT, D_MODEL // n_k)
    operands, out_specs, out_shape = [h, gains, wu_bf, wd_bf], [row_spec], [h_shape]
    if emit_u:
        in_specs.append(_layer_row(layer + 1))
        operands.append(mix_gains)
        out_specs.append(row_spec)
        out_shape.append(jax.ShapeDtypeStruct((n_tok, D_MODEL), BF16))
    in_specs += [pl.BlockSpec((None,) + up_blk, lambda i, k: (layer + 1,) + cast_blk(i, k)),
                 pl.BlockSpec((None,) + down_blk, lambda i, k: (layer + 1,) + cast_blk(i, k))]
    operands += [w_up_f32, w_down_f32]
    out_specs += [pl.BlockSpec(up_blk, cast_blk), pl.BlockSpec(down_blk, cast_blk)]
    out_shape += [jax.ShapeDtypeStruct((D_MODEL, D_FF), BF16), jax.ShapeDtypeStruct((D_FF, D_MODEL), BF16)]
    outs = _call(functools.partial(_ffn_kernel, emit_u=emit_u, cast_next=True),
                 name="ffn", grid=grid, sem=("arbitrary", "arbitrary"),
                 in_specs=in_specs, out_specs=out_specs, out_shape=out_shape,
                 operands=tuple(operands), scratch_shapes=scratch)
    return outs[0], (outs[1] if emit_u else None), tuple(outs[-2:])


_NT = (((1,), (1,)), ((), ()))


def _mlstm_in_kernel(u_ref, w_ref, wg_ref, bg_ref, p_ref, gc_ref, w_scr):
    jj, i = pl.program_id(0), pl.program_id(1)

    @pl.when(i == 0)
    def _():
        w_scr[...] = w_ref[...].astype(BF16)

    u = u_ref[...]
    p_ref[...] = lax.dot_general(u, w_scr[...], _NT, preferred_element_type=F32).astype(BF16)

    @pl.when(jj == 0)
    def _():
        pre = lax.dot_general(u, wg_ref[...].astype(BF16), _NT, preferred_element_type=F32) + bg_ref[...]
        capped = GATE_CAP * jnp.tanh(pre / GATE_CAP)
        log_f = jnp.minimum(capped, 0.0) - jnp.log1p(jnp.exp(-jnp.abs(capped)))
        lane = lax.broadcasted_iota(jnp.int32, capped.shape, 1)
        gc_ref[...] = jnp.where(lane < HEADS, capped, log_f)


def _mlstm_in(j, u, w_t, wg_t, b_gate):
    n_tok = u.shape[0]
    tm, nb = 1536, 1024
    n_i = n_tok // tm
    return _call(
        _mlstm_in_kernel, name="mlstm_in", grid=(QKVO_DIM // nb, n_i), sem=("arbitrary", "arbitrary"),
        in_specs=[pl.BlockSpec((tm, D_MODEL), lambda jj, i: (i, 0)),
                  pl.BlockSpec((None, nb, D_MODEL), lambda jj, i: (j, jj, 0)),
                  pl.BlockSpec((None, LANES, D_MODEL), lambda jj, i: (j, 0, 0)),
                  _layer_row(j, LANES)],
        out_specs=[pl.BlockSpec((tm, nb), lambda jj, i: (i, jj)),
                   pl.BlockSpec((tm, LANES), lambda jj, i: (jnp.where(jj == 0, i, n_i - 1), 0))],
        out_shape=[jax.ShapeDtypeStruct((n_tok, QKVO_DIM), BF16),
                   jax.ShapeDtypeStruct((n_tok, LANES), F32)],
        operands=(u, w_t, wg_t, b_gate),
        scratch_shapes=[pltpu.VMEM((nb, D_MODEL), BF16)])


_TN = (((0,), (0,)), ((), ()))


def _gated_heads(num, den, m_row, o, hn):
    inv = 1.0 / jnp.maximum(jnp.abs(den), jnp.exp(-m_row))
    scale = inv * lax.rsqrt(jnp.mean(num * num, axis=-1, keepdims=True) * (inv * inv) + EPS)
    return (num * scale * (hn * jax.nn.sigmoid(o.astype(F32)))).astype(BF16)


def _bf16_terms(x):
    hi = x.astype(BF16)
    rest = x - hi.astype(F32)
    mid = rest.astype(BF16)
    return hi, mid, (rest - mid.astype(F32)).astype(BF16)


def _select_sum(sel, terms, dims, sel_is_lhs):
    sel = sel.astype(BF16)
    dot = lambda t: lax.dot_general(*((sel, t) if sel_is_lhs else (t, sel)), (dims, ((), ())),
                                    preferred_element_type=F32)
    return dot(terms[0]) + dot(terms[1]) + dot(terms[2])


def _gate_sums(gates_c, causal, causal_t):
    pick = (lax.broadcasted_iota(jnp.int32, (2 * HEADS, LANES), 0)
            == lax.broadcasted_iota(jnp.int32, (2 * HEADS, LANES), 1))
    terms_c = _bf16_terms(gates_c)
    gates_r = _select_sum(pick, terms_c, ((1,), (1,)), True)
    cum_c = _select_sum(causal, terms_c, ((1,), (0,)), True)
    cum_r = _select_sum(causal_t, _bf16_terms(gates_r), ((1,), (0,)), False)
    return gates_r, cum_c, cum_r


def _mlstm_prompt_chunk(q_ref, k_ref, v_ref, o_ref, gc_ref, hn_ref, y_ref, c_ref, n_ref, m_ref, *, chunk):
    @pl.when(pl.program_id(1) == 0)
    def _():
        c_ref[...] = jnp.zeros(c_ref.shape, F32)
        n_ref[...] = jnp.zeros(n_ref.shape, F32)
        m_ref[...] = jnp.zeros(m_ref.shape, F32)

    head_row = lax.broadcasted_iota(jnp.int32, (HEADS, DK), 0)
    head_lane = lax.broadcasted_iota(jnp.int32, (1, LANES), 1)
    row = lax.broadcasted_iota(jnp.int32, (chunk, chunk), 0)
    col = lax.broadcasted_iota(jnp.int32, (chunk, chunk), 1)
    causal = col <= row
    gates_c = gc_ref[...]
    gates_r, cum_c, cum_r = _gate_sums(gates_c, causal, row <= col)
    n_all = n_ref[0]
    m_all = m_ref[0]
    n_out = jnp.zeros((HEADS, DK), F32)
    m_out = jnp.zeros((1, LANES), F32)

    for hd in range(HEADS):
        q = q_ref[:, hd * DK:(hd + 1) * DK] * BF16(DK ** -0.5)
        k = k_ref[:, hd * DK:(hd + 1) * DK]
        v = v_ref[:, hd * DV:(hd + 1) * DV]
        i_col = gates_c[:, hd:hd + 1]
        i_row = gates_r[hd:hd + 1, :]
        b_col = cum_c[:, HEADS + hd:HEADS + hd + 1]
        b_row = cum_r[HEADS + hd:HEADS + hd + 1, :]
        c_prev = c_ref[0, hd]
        m_prev = m_all[:, hd:hd + 1]

        a = b_col + m_prev
        dm = jnp.where(causal, b_col - b_row + i_row, -jnp.inf)
        m_row = jnp.maximum(a, jnp.max(dm, axis=1, keepdims=True))
        w_intra = jnp.exp(dm - m_row)
        w_inter = jnp.exp(a - m_row)
        s = lax.dot_general(q, k, _NT, preferred_element_type=F32) * w_intra
        num = (w_inter * jnp.dot(q, c_prev.astype(BF16), preferred_element_type=F32)
               + jnp.dot(s.astype(BF16), v, preferred_element_type=F32))
        den = (w_inter * jnp.sum(q.astype(F32) * n_all[hd:hd + 1, :], axis=1, keepdims=True)
               + jnp.sum(s, axis=1, keepdims=True))

        b_last = b_col[chunk - 1:chunk, :]
        m_new = m_row[chunk - 1:chunk, :]
        decay = jnp.exp(b_last + m_prev - m_new)
        kw = k.astype(F32) * jnp.exp(b_last - b_col + i_col - m_new)
        c_ref[0, hd] = decay * c_prev + lax.dot_general(kw.astype(BF16), v, _TN, preferred_element_type=F32)
        n_new = decay * n_all[hd:hd + 1, :] + jnp.sum(kw, axis=0, keepdims=True)
        n_out = jnp.where(head_row == hd, n_new, n_out)
        m_out = jnp.where(head_lane == hd, m_new, m_out)

        y_ref[:, hd * DV:(hd + 1) * DV] = _gated_heads(
            num, den, m_row, o_ref[:, hd * DV:(hd + 1) * DV], hn_ref[:, hd * DV:(hd + 1) * DV])

    n_ref[0] = n_out
    m_ref[0] = m_out


def _mlstm_sample_group(q_ref, k_ref, v_ref, o_ref, gc_ref, hn_ref, c0_ref, n0_ref, m0_ref,
                        y_ref, c_ref, n_ref, m_ref, *, t_len, n_sub):
    n_rows = n_sub * t_len
    row = lax.broadcasted_iota(jnp.int32, (n_rows, n_rows), 0)
    col = lax.broadcasted_iota(jnp.int32, (n_rows, n_rows), 1)
    seq_start = lax.broadcasted_iota(jnp.int32, (n_sub, t_len, n_rows), 0).reshape(n_rows, n_rows) * t_len
    same_seq = (col >= seq_start) & (col < seq_start + t_len)
    causal = same_seq & (col <= row)
    causal_t = same_seq & (row <= col)

    def per_seq(x):
        return jnp.broadcast_to(x, (n_sub, t_len, x.shape[-1])).reshape(n_rows, x.shape[-1])

    def last_token(x):
        return per_seq(x.reshape(n_sub, t_len, x.shape[-1])[:, t_len - 1:t_len, :])

    gates_c = gc_ref[...]
    gates_r, cum_c, cum_r = _gate_sums(gates_c, causal, causal_t)
    m_prev_all = per_seq(m0_ref[...])
    b_last_all = last_token(cum_c)
    head_row = lax.broadcasted_iota(jnp.int32, (HEADS, DK), 0)
    head_lane = lax.broadcasted_iota(jnp.int32, (1, LANES), 1)
    n_out = [jnp.zeros((HEADS, DK), F32)] * n_sub
    m_out = [jnp.zeros((1, LANES), F32)] * n_sub

    for hd in range(HEADS):
        q = q_ref[:, hd * DK:(hd + 1) * DK] * BF16(DK ** -0.5)
        k = k_ref[:, hd * DK:(hd + 1) * DK]
        v = v_ref[:, hd * DV:(hd + 1) * DV]
        i_col = gates_c[:, hd:hd + 1]
        i_row = gates_r[hd:hd + 1, :]
        b_col = cum_c[:, HEADS + hd:HEADS + hd + 1]
        b_row = cum_r[HEADS + hd:HEADS + hd + 1, :]
        m_prev = m_prev_all[:, hd:hd + 1]
        b_last = b_last_all[:, HEADS + hd:HEADS + hd + 1]

        a = b_col + m_prev
        dm = jnp.where(causal, b_col - b_row + i_row, -jnp.inf)
        m_row = jnp.maximum(a, jnp.max(dm, axis=1, keepdims=True))
        w_intra = jnp.exp(dm - m_row)
        w_inter = jnp.exp(a - m_row)
        s = lax.dot_general(q, k, _NT, preferred_element_type=F32) * w_intra
        inter = jnp.concatenate(
            [jnp.dot(q_ref[sub * t_len:(sub + 1) * t_len, hd * DK:(hd + 1) * DK] * BF16(DK ** -0.5),
                     c0_ref[sub, hd].astype(BF16), preferred_element_type=F32)
             for sub in range(n_sub)], axis=0)
        num = w_inter * inter + jnp.dot(s.astype(BF16), v, preferred_element_type=F32)
        n_prev = per_seq(n0_ref[:, hd:hd + 1, :])
        den = (w_inter * jnp.sum(q.astype(F32) * n_prev, axis=1, keepdims=True)
               + jnp.sum(s, axis=1, keepdims=True))
        y_ref[:, hd * DV:(hd + 1) * DV] = _gated_heads(
            num, den, m_row, o_ref[:, hd * DV:(hd + 1) * DV], hn_ref[:, hd * DV:(hd + 1) * DV])

        m_new = last_token(m_row)
        decay = jnp.exp(b_last + m_prev - m_new)
        kw = k.astype(F32) * jnp.exp(b_last - b_col + i_col - m_new)
        for sub in range(n_sub):
            rows = slice(sub * t_len, (sub + 1) * t_len)
            last = slice((sub + 1) * t_len - 1, (sub + 1) * t_len)
            c_ref[sub, hd] = decay[last] * c0_ref[sub, hd] + lax.dot_general(
                kw[rows].astype(BF16), v_ref[rows, hd * DV:(hd + 1) * DV], _TN, preferred_element_type=F32)
            n_new = decay[last] * n0_ref[sub, hd:hd + 1, :] + jnp.sum(kw[rows], axis=0, keepdims=True)
            n_out[sub] = jnp.where(head_row == hd, n_new, n_out[sub])
            m_out[sub] = jnp.where(head_lane == hd, m_new[last], m_out[sub])

    for sub in range(n_sub):
        n_ref[sub] = n_out[sub]
        m_ref[sub] = m_out[sub]


N_PROJ_REFS = 5
REC_CHUNK = 256
REC_GROUP = 4


def _mlstm_rec_kernel(*refs, chunk, t_len, n_sub, n_carried):
    p_in, s_in = refs[:N_PROJ_REFS], refs[N_PROJ_REFS:2 * N_PROJ_REFS]
    hn_ref, c0_ref, n0_ref, m0_ref = refs[2 * N_PROJ_REFS:2 * N_PROJ_REFS + 4]
    outs = refs[2 * N_PROJ_REFS + 4 + n_carried:]
    yp_ref, ys_ref = outs[:2]
    _mlstm_prompt_chunk(*p_in, hn_ref, yp_ref, *outs[2:5], chunk=chunk)
    _mlstm_sample_group(*s_in, hn_ref, c0_ref, n0_ref, m0_ref, ys_ref, *outs[5:8], t_len=t_len, n_sub=n_sub)


def _mlstm_rec(j, proj, gates_c, head_norm, state_s, new_p, new_s, dims):
    batch, seq, dec_batch, t_len = dims
    n_tok = proj.shape[0]
    chunk, n_sub = REC_CHUNK, REC_GROUP
    n_chunks = seq // chunk
    assert batch * n_chunks == dec_batch // n_sub
    s_blk = n_sub * t_len
    s_base = batch * seq // s_blk

    def proj_specs(blk, base):
        rows = lambda width_blk: (lambda b, c: (base + b * n_chunks + c, width_blk))
        return [pl.BlockSpec((blk, QK_DIM), rows(0)), pl.BlockSpec((blk, QK_DIM), rows(1)),
                pl.BlockSpec((blk, V_DIM), rows(1)), pl.BlockSpec((blk, V_DIM), rows(2)),
                pl.BlockSpec((blk, LANES), rows(0))]

    def state_specs(n_seq_blk, seq_blk_of):
        return [pl.BlockSpec((None, n_seq_blk, HEADS, DK, DV), lambda b, c: (j, seq_blk_of(b, c), 0, 0, 0)),
                pl.BlockSpec((None, n_seq_blk, HEADS, DK), lambda b, c: (j, seq_blk_of(b, c), 0, 0)),
                pl.BlockSpec((None, n_seq_blk, 1, LANES), lambda b, c: (j, seq_blk_of(b, c), 0, 0))]

    def state_shapes(n_seq):
        return [jax.ShapeDtypeStruct((N_MIX, n_seq, HEADS, DK, DV), F32),
                jax.ShapeDtypeStruct((N_MIX, n_seq, HEADS, DK), F32),
                jax.ShapeDtypeStruct((N_MIX, n_seq, 1, LANES), F32)]

    p_state = state_specs(1, lambda b, c: b)
    s_state = state_specs(n_sub, lambda b, c: b * n_chunks + c)
    carried = {}
    if new_p is not None:
        carried.update({2 + i: a for i, a in enumerate(new_p)})
        carried.update({5 + i: a for i, a in enumerate(new_s)})
    proj_ops = (proj, proj, proj, proj, gates_c)
    outs = _call(
        functools.partial(_mlstm_rec_kernel, chunk=chunk, t_len=t_len, n_sub=n_sub, n_carried=len(carried)),
        name="mlstm_rec", grid=(batch, n_chunks), sem=("arbitrary", "arbitrary"),
        in_specs=proj_specs(chunk, 0) + proj_specs(s_blk, s_base) + [_layer_row(j)] + s_state,
        out_specs=[pl.BlockSpec((chunk, V_DIM), lambda b, c: (b * n_chunks + c, 0)),
                   pl.BlockSpec((s_blk, V_DIM), lambda b, c: (b * n_chunks + c, 0))] + p_state + s_state,
        out_shape=[jax.ShapeDtypeStruct((batch * seq, V_DIM), BF16),
                   jax.ShapeDtypeStruct((dec_batch * t_len, V_DIM), BF16)]
                  + state_shapes(batch) + state_shapes(dec_batch),
        operands=proj_ops + proj_ops + (head_norm,) + tuple(state_s),
        carried=carried)
    return outs[0], outs[1], tuple(outs[2:5]), tuple(outs[5:8])


def _out_proj_kernel(x_ref, yp_ref, ys_ref, w_ref, o_ref, *, n_prompt_tiles):
    y = jnp.where(pl.program_id(0) < n_prompt_tiles, yp_ref[...], ys_ref[...])
    o_ref[...] = x_ref[...] + jnp.dot(y, w_ref[...], preferred_element_type=F32)


def _out_proj(j, h, y_prompt, y_sample, w_bf, n_prompt):
    n_tok = h.shape[0]
    tm = 512
    npt = n_prompt // tm
    row_spec = pl.BlockSpec((tm, D_MODEL), lambda i: (i, 0))
    return _call(
        functools.partial(_out_proj_kernel, n_prompt_tiles=npt),
        name="mlstm_out", grid=(n_tok // tm,), sem=("arbitrary",),
        in_specs=[row_spec,
                  pl.BlockSpec((tm, V_DIM), lambda i: (jnp.minimum(i, npt - 1), 0)),
                  pl.BlockSpec((tm, V_DIM), lambda i: (jnp.maximum(i - npt, 0), 0)),
                  pl.BlockSpec((None, V_DIM, D_MODEL), lambda i: (j, 0, 0))],
        out_specs=row_spec, out_shape=jax.ShapeDtypeStruct((n_tok, D_MODEL), F32),
        operands=(h, y_prompt, y_sample, w_bf))


def _mlstm_layer(j, h, u, state_s, new_p, new_s, w_t, wg_t, b_gate, head_norm, w_out_bf, dims):
    proj, gates_c = _mlstm_in(j, u, w_t, wg_t, b_gate)
    y_prompt, y_sample, new_p, new_s = _mlstm_rec(j, proj, gates_c, head_norm, state_s, new_p, new_s, dims)
    return _out_proj(j, h, y_prompt, y_sample, w_out_bf, dims[0] * dims[1]), new_p, new_s


def kernel(x_prompt, x_sample, state_pool, state_mlstm_C, state_mlstm_n, state_mlstm_m, norm_mix, norm_ffn, norm_final, pool_w, pool_scale, mlstm_w_in, mlstm_b_gate, mlstm_head_norm, mlstm_w_out, ffn_w_up, ffn_w_down):
    batch, seq, _ = x_prompt.shape
    dec_batch, t_len, _ = x_sample.shape
    dims = (batch, seq, dec_batch, t_len)
    n_prompt = batch * seq
    x_prompt = x_prompt.reshape(n_prompt, D_MODEL)
    x_sample = x_sample.reshape(dec_batch * t_len, D_MODEL)

    rows3 = lambda a: a.reshape(a.shape[0], 1, a.shape[1])
    norm_mix, norm_ffn, pool_scale, head_norm = map(rows3, (norm_mix, norm_ffn, pool_scale, mlstm_head_norm))
    pad_lanes = lambda a: jnp.pad(a, [(0, 0)] * (a.ndim - 1) + [(0, LANES - a.shape[-1])])
    b_gate = rows3(pad_lanes(mlstm_b_gate))
    state_s = (state_mlstm_C, state_mlstm_n, pad_lanes(state_mlstm_m)[:, :, None, :])
    pool_w_bf = pool_w.astype(BF16)
    w_in_t = jnp.swapaxes(mlstm_w_in, 1, 2)
    w_gate_t = jnp.pad(w_in_t[:, QKVO_DIM:, :], ((0, 0), (0, LANES - 2 * HEADS), (0, 0)))
    w_out_bf = mlstm_w_out.astype(BF16)
    ffn_bf = (ffn_w_up[0].astype(BF16), ffn_w_down[0].astype(BF16))
    norm_final = norm_final.reshape(1, D_MODEL)

    h = u = new_pool = new_p = new_s = None
    for layer in range(DEPTH):
        j = layer // 2
        if layer % 2 == 0:
            h, new_pool = _pool_layer(j, layer, h, x_prompt, x_sample, state_pool, new_pool,
                                      norm_mix, pool_w_bf, pool_scale, dims)
        else:
            h, new_p, new_s = _mlstm_layer(j, h, u, state_s, new_p, new_s, w_in_t, w_gate_t,
                                           b_gate, head_norm, w_out_bf, dims)
        h, u, ffn_bf = _ffn_layer(layer, h, norm_ffn, *ffn_bf, ffn_w_up, ffn_w_down, norm_mix, norm_final,
                                  n_prompt)

    y_prompt = h[0].reshape(batch, seq, D_MODEL)
    y_sample = h[1].reshape(dec_batch, t_len, D_MODEL)
    return (y_prompt, y_sample, new_pool[0], new_pool[1],
            new_p[0], new_p[1], new_p[2][:, :, 0, :HEADS],
            new_s[0], new_s[1], new_s[2][:, :, 0, :HEADS])
```

```python
import functools

import jax
import jax.numpy as jnp
from jax import lax
from jax.experimental import pallas as pl
from jax.experimental.pallas import tpu as pltpu

F32 = jnp.float32
BF16 = jnp.bfloat16

D_MODEL = 2048
DEPTH = 4
N_MIX = DEPTH // 2
POOL_WINDOWS = (2, 4, 8, 16)
N_GROUPS = len(POOL_WINDOWS)
GROUP_DIM = D_MODEL // N_GROUPS
POOL_BUF = max(POOL_WINDOWS) - 1
assert POOL_WINDOWS[0] == 2 and all(b == 2 * a for a, b in zip(POOL_WINDOWS, POOL_WINDOWS[1:]))
HALO = POOL_BUF + 1
PAD = 8
HEADS = 4
QK_DIM = D_MODEL // 2
V_DIM = D_MODEL
DK = QK_DIM // HEADS
DV = V_DIM // HEADS
QKVO_DIM = 2 * QK_DIM + 2 * V_DIM
GATE_CAP = 15.0
D_FF = 4 * D_MODEL
EPS = 1e-6
LANES = 128

VMEM_LIMIT = 60 * 1024 * 1024


def _rmsnorm(x, g):
    return x * lax.rsqrt(jnp.mean(x * x, axis=-1, keepdims=True) + EPS) * g


def _call(body, *, name, grid, sem, in_specs, out_specs, out_shape, operands, carried=None, aliases=None,
          scratch_shapes=()):
    carried = carried or {}
    aliases = dict(aliases or {})
    for pos, out_idx in enumerate(carried):
        aliases[len(operands) + pos] = out_idx
    return pl.pallas_call(
        body,
        grid=grid,
        in_specs=list(in_specs) + [pl.BlockSpec(memory_space=pl.ANY)] * len(carried),
        out_specs=out_specs,
        out_shape=out_shape,
        scratch_shapes=list(scratch_shapes),
        input_output_aliases=aliases,
        compiler_params=pltpu.CompilerParams(dimension_semantics=sem, vmem_limit_bytes=VMEM_LIMIT),
        name=name,
    )(*operands, *carried.values())


def _layer_row(layer, d=D_MODEL):
    return pl.BlockSpec((None, 1, d), lambda *_: (layer, 0, 0))


def _pool_prompt_kernel(h_ref, g_ref, w_ref, s_ref, *rest, tt, n_carried):
    o_ref, nb_ref, ext_ref, *lvl_refs = rest[n_carried:]
    t = pl.program_id(1)
    base = PAD + HALO
    end = base + tt

    @pl.when(t == 0)
    def _():
        ext_ref[0:base, :] = jnp.zeros((base, D_MODEL), F32)
        for ref in lvl_refs:
            ref[0:PAD, :] = jnp.zeros((PAD, D_MODEL), F32)

    x = h_ref[...]
    ext_ref[base:end, :] = _rmsnorm(x, g_ref[...])
    n_seen = t * tt + 1 + lax.broadcasted_iota(jnp.int32, (tt, 1), 0)
    src = ext_ref
    outs = []
    for g, w in enumerate(POOL_WINDOWS):
        half = w // 2
        c0 = g * GROUP_DIM
        cs = slice(c0, c0 + GROUP_DIM)
        if g < len(lvl_refs):
            dst = lvl_refs[g]
            dst[PAD:end, c0:] = src[PAD:end, c0:] + src[PAD - half:end - half, c0:]
            s = dst[base:end, cs]
            src = dst
        else:
            s = src[base:end, cs] + src[base - half:end - half, cs]
        cur = ext_ref[base:end, cs]
        d = s / jnp.minimum(n_seen, w).astype(F32) - cur
        outs.append(jnp.dot(d.astype(BF16), w_ref[g], preferred_element_type=F32))
    o_ref[...] = x + jnp.concatenate(outs, axis=-1) * s_ref[...]

    @pl.when(t == pl.num_programs(1) - 1)
    def _():
        nb_ref[0] = ext_ref[end - POOL_BUF:end, :]

    ext_ref[PAD:base, :] = ext_ref[end - HALO:end, :]


def _pool_sample_kernel(h_ref, buf_ref, g_ref, w_ref, s_ref, *rest, sb, t_len, n_carried):
    o_ref, nb_ref, ext_ref = rest[n_carried:]
    x = h_ref[...]
    u = _rmsnorm(x, g_ref[...])
    ext_ref[:, 1:HALO, :] = buf_ref[...]
    ext_ref[:, HALO:HALO + t_len, :] = u.reshape(sb, t_len, D_MODEL)
    outs = []
    for g, w in enumerate(POOL_WINDOWS):
        cs = slice(g * GROUP_DIM, (g + 1) * GROUP_DIM)
        cur = ext_ref[:, HALO:HALO + t_len, cs]
        s = cur
        for i in range(1, w):
            s = s + ext_ref[:, HALO - i:HALO - i + t_len, cs]
        d = (s * (1.0 / w) - cur).reshape(sb * t_len, GROUP_DIM)
        outs.append(jnp.dot(d.astype(BF16), w_ref[g], preferred_element_type=F32))
    o_ref[...] = x + jnp.concatenate(outs, axis=-1) * s_ref[...]
    nb_ref[...] = ext_ref[:, HALO + t_len - POOL_BUF:HALO + t_len, :]


def _pool_layer(j, layer, h, x_prompt, x_sample, state_pool, new_pool, gains, w_bf, scales, dims):
    batch, seq, dec_batch, t_len = dims
    n_prompt, n_tok = batch * seq, batch * seq + dec_batch * t_len
    first = h is None
    h_shape = jax.ShapeDtypeStruct((n_tok, D_MODEL), F32)
    w_spec = pl.BlockSpec((None, N_GROUPS, GROUP_DIM, GROUP_DIM), lambda *_: (j, 0, 0, 0))

    tt = 512
    nt = seq // tt
    row_spec = pl.BlockSpec((tt, D_MODEL), lambda b, t: (b * nt + t, 0))
    h, nb_p = _call(
        functools.partial(_pool_prompt_kernel, tt=tt, n_carried=0 if first else 1),
        name="pool_prompt", grid=(batch, nt), sem=("arbitrary", "arbitrary"),
        in_specs=[row_spec, _layer_row(layer), w_spec, _layer_row(j)],
        out_specs=[row_spec, pl.BlockSpec((None, 1, POOL_BUF, D_MODEL), lambda b, t: (j, b, 0, 0))],
        out_shape=[h_shape, jax.ShapeDtypeStruct((N_MIX, batch, POOL_BUF, D_MODEL), F32)],
        operands=(x_prompt if first else h, gains, w_bf, scales),
        carried=None if first else {1: new_pool[0]},
        aliases=None if first else {0: 0},
        scratch_shapes=[pltpu.VMEM((PAD + HALO + tt, D_MODEL), F32)] * N_GROUPS)

    sb = 16
    rows = sb * t_len
    base = n_prompt // rows
    out_rows = pl.BlockSpec((rows, D_MODEL), lambda i: (base + i, 0))
    h, nb_s = _call(
        functools.partial(_pool_sample_kernel, sb=sb, t_len=t_len, n_carried=1),
        name="pool_sample", grid=(dec_batch // sb,), sem=("arbitrary",),
        in_specs=[pl.BlockSpec((rows, D_MODEL), lambda i: (i, 0)) if first else out_rows,
                  pl.BlockSpec((None, sb, POOL_BUF, D_MODEL), lambda i: (j, i, 0, 0)),
                  _layer_row(layer), w_spec, _layer_row(j)],
        out_specs=[out_rows, pl.BlockSpec((None, sb, POOL_BUF, D_MODEL), lambda i: (j, i, 0, 0))],
        out_shape=[h_shape, jax.ShapeDtypeStruct((N_MIX, dec_batch, POOL_BUF, D_MODEL), F32)],
        operands=(x_sample if first else h, state_pool, gains, w_bf, scales),
        carried={0: h} if first else {1: new_pool[1]},
        aliases=None if first else {0: 0},
        scratch_shapes=[pltpu.VMEM((sb, HALO + t_len, D_MODEL), F32)])
    return h, (nb_p, nb_s)


def _ffn_tile(x_ref, g_ref, wu_ref, wd_ref, o_ref, u_ref):
    @pl.when(pl.program_id(1) == 0)
    def _():
        x = x_ref[...]
        u_ref[...] = _rmsnorm(x, g_ref[...]).astype(BF16)
        o_ref[...] = x

    a = jnp.maximum(jnp.dot(u_ref[...], wu_ref[...], preferred_element_type=F32), 0.0)
    o_ref[...] += jnp.dot((a * a).astype(BF16), wd_ref[...], preferred_element_type=F32)


def _ffn_kernel(x_ref, g_ref, wu_ref, wd_ref, *rest, emit_u, cast_next):
    rest = list(rest)
    gn_ref = rest.pop(0) if emit_u else None
    wun_ref, wdn_ref = (rest.pop(0), rest.pop(0)) if cast_next else (None, None)
    o_ref = rest.pop(0)
    un_ref = rest.pop(0) if emit_u else None
    if cast_next:
        wun_bf_ref, wdn_bf_ref = rest.pop(0), rest.pop(0)
        wun_bf_ref[...] = wun_ref[...].astype(BF16)
        wdn_bf_ref[...] = wdn_ref[...].astype(BF16)
    (u_ref,) = rest
    _ffn_tile(x_ref, g_ref, wu_ref, wd_ref, o_ref, u_ref)

    if emit_u:
        @pl.when(pl.program_id(1) == pl.num_programs(1) - 1)
        def _():
            un_ref[...] = _rmsnorm(o_ref[...], gn_ref[...]).astype(BF16)


def _ffn_final_kernel(x_ref, g_ref, wu_ref, wd_ref, gf_ref, yp_ref, ys_ref, u_ref, *, n_prompt_tiles):
    def tile(o_ref):
        _ffn_tile(x_ref, g_ref, wu_ref, wd_ref, o_ref, u_ref)

        @pl.when(pl.program_id(1) == pl.num_programs(1) - 1)
        def _():
            o_ref[...] = _rmsnorm(o_ref[...], gf_ref[...])

    is_prompt = pl.program_id(0) < n_prompt_tiles
    pl.when(is_prompt)(lambda: tile(yp_ref))
    pl.when(jnp.logical_not(is_prompt))(lambda: tile(ys_ref))


FFN_TM, FFN_FC = 768, 1024
FFN_FINAL_TILE = (512, 1024)
CAST_SPLIT = 8


def _ffn_layer(layer, h, gains, wu_bf, wd_bf, w_up_f32, w_down_f32, mix_gains, gain_final, n_prompt):
    n_tok = h.shape[0]
    tm, fc = FFN_FINAL_TILE if layer == DEPTH - 1 else (FFN_TM, FFN_FC)
    n_k = D_FF // fc
    row_spec = pl.BlockSpec((tm, D_MODEL), lambda i, k: (i, 0))
    in_specs = [row_spec, _layer_row(layer),
                pl.BlockSpec((D_MODEL, fc), lambda i, k: (0, k)),
                pl.BlockSpec((fc, D_MODEL), lambda i, k: (k, 0))]
    scratch = [pltpu.VMEM((tm, D_MODEL), BF16)]
    grid = (n_tok // tm, n_k)
    h_shape = jax.ShapeDtypeStruct((n_tok, D_MODEL), F32)
    if layer == DEPTH - 1:
        npt = n_prompt // tm
        ys = _call(
            functools.partial(_ffn_final_kernel, n_prompt_tiles=npt),
            name="ffn_final", grid=grid, sem=("arbitrary", "arbitrary"),
            in_specs=in_specs + [pl.BlockSpec((1, D_MODEL), lambda i, k: (0, 0))],
            out_specs=[pl.BlockSpec((tm, D_MODEL), lambda i, k: (jnp.minimum(i, npt - 1), 0)),
                       pl.BlockSpec((tm, D_MODEL), lambda i, k: (jnp.maximum(i - npt, 0), 0))],
            out_shape=[jax.ShapeDtypeStruct((n_prompt, D_MODEL), F32),
                       jax.ShapeDtypeStruct((n_tok - n_prompt, D_MODEL), F32)],
            operands=(h, gains, wu_bf, wd_bf, gain_final), scratch_shapes=scratch)
        return ys, None, None

    emit_u = layer % 2 == 0
    cast_blk = lambda i, k: (jnp.minimum(i, CAST_SPLIT - 1), jnp.where(i < CAST_SPLIT, k, n_k - 1))
    up_blk, down_blk = (D_MODEL // CAST_SPLIT, fc), (D_FF // CAST_SPLIT, D_MODEL // n_k)
    operands, out_specs, out_shape = [h, gains, wu_bf, wd_bf], [row_spec], [h_shape]
    if emit_u:
        in_specs.append(_layer_row(layer + 1))
        operands.append(mix_gains)
        out_specs.append(row_spec)
        out_shape.append(jax.ShapeDtypeStruct((n_tok, D_MODEL), BF16))
    in_specs += [pl.BlockSpec((None,) + up_blk, lambda i, k: (layer + 1,) + cast_blk(i, k)),
                 pl.BlockSpec((None,) + down_blk, lambda i, k: (layer + 1,) + cast_blk(i, k))]
    operands += [w_up_f32, w_down_f32]
    out_specs += [pl.BlockSpec(up_blk, cast_blk), pl.BlockSpec(down_blk, cast_blk)]
    out_shape += [jax.ShapeDtypeStruct((D_MODEL, D_FF), BF16), jax.ShapeDtypeStruct((D_FF, D_MODEL), BF16)]
    outs = _call(functools.partial(_ffn_kernel, emit_u=emit_u, cast_next=True),
                 name="ffn", grid=grid, sem=("arbitrary", "arbitrary"),
                 in_specs=in_specs, out_specs=out_specs, out_shape=out_shape,
                 operands=tuple(operands), scratch_shapes=scratch)
    return outs[0], (outs[1] if emit_u else None), tuple(outs[-2:])


_NT = (((1,), (1,)), ((), ()))


def _mlstm_in_kernel(u_ref, w_ref, wg_ref, bg_ref, p_ref, gc_ref, w_scr):
    jj, i = pl.program_id(0), pl.program_id(1)

    @pl.when(i == 0)
    def _():
        w_scr[...] = w_ref[...].astype(BF16)

    u = u_ref[...]
    p_ref[...] = lax.dot_general(u, w_scr[...], _NT, preferred_element_type=F32).astype(BF16)

    @pl.when(jj == 0)
    def _():
        pre = lax.dot_general(u, wg_ref[...].astype(BF16), _NT, preferred_element_type=F32) + bg_ref[...]
        capped = GATE_CAP * jnp.tanh(pre / GATE_CAP)
        log_f = jnp.minimum(capped, 0.0) - jnp.log1p(jnp.exp(-jnp.abs(capped)))
        lane = lax.broadcasted_iota(jnp.int32, capped.shape, 1)
        gc_ref[...] = jnp.where(lane < HEADS, capped, log_f)


def _mlstm_in(j, u, w_t, wg_t, b_gate):
    n_tok = u.shape[0]
    tm, nb = 1536, 1024
    n_i = n_tok // tm
    return _call(
        _mlstm_in_kernel, name="mlstm_in", grid=(QKVO_DIM // nb, n_i), sem=("arbitrary", "arbitrary"),
        in_specs=[pl.BlockSpec((tm, D_MODEL), lambda jj, i: (i, 0)),
                  pl.BlockSpec((None, nb, D_MODEL), lambda jj, i: (j, jj, 0)),
                  pl.BlockSpec((None, LANES, D_MODEL), lambda jj, i: (j, 0, 0)),
                  _layer_row(j, LANES)],
        out_specs=[pl.BlockSpec((tm, nb), lambda jj, i: (i, jj)),
                   pl.BlockSpec((tm, LANES), lambda jj, i: (jnp.where(jj == 0, i, n_i - 1), 0))],
        out_shape=[jax.ShapeDtypeStruct((n_tok, QKVO_DIM), BF16),
                   jax.ShapeDtypeStruct((n_tok, LANES), F32)],
        operands=(u, w_t, wg_t, b_gate),
        scratch_shapes=[pltpu.VMEM((nb, D_MODEL), BF16)])


_TN = (((0,), (0,)), ((), ()))


def _gated_heads(num, den, m_row, o, hn):
    inv = 1.0 / jnp.maximum(jnp.abs(den), jnp.exp(-m_row))
    scale = inv * lax.rsqrt(jnp.mean(num * num, axis=-1, keepdims=True) * (inv * inv) + EPS)
    return (num * scale * (hn * jax.nn.sigmoid(o.astype(F32)))).astype(BF16)


def _bf16_terms(x):
    hi = x.astype(BF16)
    rest = x - hi.astype(F32)
    mid = rest.astype(BF16)
    return hi, mid, (rest - mid.astype(F32)).astype(BF16)


def _select_sum(sel, terms, dims, sel_is_lhs):
    sel = sel.astype(BF16)
    dot = lambda t: lax.dot_general(*((sel, t) if sel_is_lhs else (t, sel)), (dims, ((), ())),
                                    preferred_element_type=F32)
    return dot(terms[0]) + dot(terms[1]) + dot(terms[2])


def _gate_sums(gates_c, causal, causal_t):
    pick = (lax.broadcasted_iota(jnp.int32, (2 * HEADS, LANES), 0)
            == lax.broadcasted_iota(jnp.int32, (2 * HEADS, LANES), 1))
    terms_c = _bf16_terms(gates_c)
    gates_r = _select_sum(pick, terms_c, ((1,), (1,)), True)
    cum_c = _select_sum(causal, terms_c, ((1,), (0,)), True)
    cum_r = _select_sum(causal_t, _bf16_terms(gates_r), ((1,), (0,)), False)
    return gates_r, cum_c, cum_r


def _mlstm_prompt_chunk(q_ref, k_ref, v_ref, o_ref, gc_ref, hn_ref, y_ref, c_ref, n_ref, m_ref, *, chunk):
    @pl.when(pl.program_id(1) == 0)
    def _():
        c_ref[...] = jnp.zeros(c_ref.shape, F32)
        n_ref[...] = jnp.zeros(n_ref.shape, F32)
        m_ref[...] = jnp.zeros(m_ref.shape, F32)

    head_row = lax.broadcasted_iota(jnp.int32, (HEADS, DK), 0)
    head_lane = lax.broadcasted_iota(jnp.int32, (1, LANES), 1)
    row = lax.broadcasted_iota(jnp.int32, (chunk, chunk), 0)
    col = lax.broadcasted_iota(jnp.int32, (chunk, chunk), 1)
    causal = col <= row
    gates_c = gc_ref[...]
    gates_r, cum_c, cum_r = _gate_sums(gates_c, causal, row <= col)
    n_all = n_ref[0]
    m_all = m_ref[0]
    n_out = jnp.zeros((HEADS, DK), F32)
    m_out = jnp.zeros((1, LANES), F32)

    for hd in range(HEADS):
        q = q_ref[:, hd * DK:(hd + 1) * DK] * BF16(DK ** -0.5)
        k = k_ref[:, hd * DK:(hd + 1) * DK]
        v = v_ref[:, hd * DV:(hd + 1) * DV]
        i_col = gates_c[:, hd:hd + 1]
        i_row = gates_r[hd:hd + 1, :]
        b_col = cum_c[:, HEADS + hd:HEADS + hd + 1]
        b_row = cum_r[HEADS + hd:HEADS + hd + 1, :]
        c_prev = c_ref[0, hd]
        m_prev = m_all[:, hd:hd + 1]

        a = b_col + m_prev
        dm = jnp.where(causal, b_col - b_row + i_row, -jnp.inf)
        m_row = jnp.maximum(a, jnp.max(dm, axis=1, keepdims=True))
        w_intra = jnp.exp(dm - m_row)
        w_inter = jnp.exp(a - m_row)
        s = lax.dot_general(q, k, _NT, preferred_element_type=F32) * w_intra
        num = (w_inter * jnp.dot(q, c_prev.astype(BF16), preferred_element_type=F32)
               + jnp.dot(s.astype(BF16), v, preferred_element_type=F32))
        den = (w_inter * jnp.sum(q.astype(F32) * n_all[hd:hd + 1, :], axis=1, keepdims=True)
               + jnp.sum(s, axis=1, keepdims=True))

        b_last = b_col[chunk - 1:chunk, :]
        m_new = m_row[chunk - 1:chunk, :]
        decay = jnp.exp(b_last + m_prev - m_new)
        kw = k.astype(F32) * jnp.exp(b_last - b_col + i_col - m_new)
        c_ref[0, hd] = decay * c_prev + lax.dot_general(kw.astype(BF16), v, _TN, preferred_element_type=F32)
        n_new = decay * n_all[hd:hd + 1, :] + jnp.sum(kw, axis=0, keepdims=True)
        n_out = jnp.where(head_row == hd, n_new, n_out)
        m_out = jnp.where(head_lane == hd, m_new, m_out)

        y_ref[:, hd * DV:(hd + 1) * DV] = _gated_heads(
            num, den, m_row, o_ref[:, hd * DV:(hd + 1) * DV], hn_ref[:, hd * DV:(hd + 1) * DV])

    n_ref[0] = n_out
    m_ref[0] = m_out


def _mlstm_sample_group(q_ref, k_ref, v_ref, o_ref, gc_ref, hn_ref, c0_ref, n0_ref, m0_ref,
                        y_ref, c_ref, n_ref, m_ref, *, t_len, n_sub):
    n_rows = n_sub * t_len
    row = lax.broadcasted_iota(jnp.int32, (n_rows, n_rows), 0)
    col = lax.broadcasted_iota(jnp.int32, (n_rows, n_rows), 1)
    seq_start = lax.broadcasted_iota(jnp.int32, (n_sub, t_len, n_rows), 0).reshape(n_rows, n_rows) * t_len
    same_seq = (col >= seq_start) & (col < seq_start + t_len)
    causal = same_seq & (col <= row)
    causal_t = same_seq & (row <= col)

    def per_seq(x):
        return jnp.broadcast_to(x, (n_sub, t_len, x.shape[-1])).reshape(n_rows, x.shape[-1])

    def last_token(x):
        return per_seq(x.reshape(n_sub, t_len, x.shape[-1])[:, t_len - 1:t_len, :])

    gates_c = gc_ref[...]
    gates_r, cum_c, cum_r = _gate_sums(gates_c, causal, causal_t)
    m_prev_all = per_seq(m0_ref[...])
    b_last_all = last_token(cum_c)
    head_row = lax.broadcasted_iota(jnp.int32, (HEADS, DK), 0)
    head_lane = lax.broadcasted_iota(jnp.int32, (1, LANES), 1)
    n_out = [jnp.zeros((HEADS, DK), F32)] * n_sub
    m_out = [jnp.zeros((1, LANES), F32)] * n_sub

    for hd in range(HEADS):
        q = q_ref[:, hd * DK:(hd + 1) * DK] * BF16(DK ** -0.5)
        k = k_ref[:, hd * DK:(hd + 1) * DK]
        v = v_ref[:, hd * DV:(hd + 1) * DV]
        i_col = gates_c[:, hd:hd + 1]
        i_row = gates_r[hd:hd + 1, :]
        b_col = cum_c[:, HEADS + hd:HEADS + hd + 1]
        b_row = cum_r[HEADS + hd:HEADS + hd + 1, :]
        m_prev = m_prev_all[:, hd:hd + 1]
        b_last = b_last_all[:, HEADS + hd:HEADS + hd + 1]

        a = b_col + m_prev
        dm = jnp.where(causal, b_col - b_row + i_row, -jnp.inf)
        m_row = jnp.maximum(a, jnp.max(dm, axis=1, keepdims=True))
        w_intra = jnp.exp(dm - m_row)
        w_inter = jnp.exp(a - m_row)
        s = lax.dot_general(q, k, _NT, preferred_element_type=F32) * w_intra
        inter = jnp.concatenate(
            [jnp.dot(q_ref[sub * t_len:(sub + 1) * t_len, hd * DK:(hd + 1) * DK] * BF16(DK ** -0.5),
                     c0_ref[sub, hd].astype(BF16), preferred_element_type=F32)
             for sub in range(n_sub)], axis=0)
        num = w_inter * inter + jnp.dot(s.astype(BF16), v, preferred_element_type=F32)
        n_prev = per_seq(n0_ref[:, hd:hd + 1, :])
        den = (w_inter * jnp.sum(q.astype(F32) * n_prev, axis=1, keepdims=True)
               + jnp.sum(s, axis=1, keepdims=True))
        y_ref[:, hd * DV:(hd + 1) * DV] = _gated_heads(
            num, den, m_row, o_ref[:, hd * DV:(hd + 1) * DV], hn_ref[:, hd * DV:(hd + 1) * DV])

        m_new = last_token(m_row)
        decay = jnp.exp(b_last + m_prev - m_new)
        kw = k.astype(F32) * jnp.exp(b_last - b_col + i_col - m_new)
        for sub in range(n_sub):
            rows = slice(sub * t_len, (sub + 1) * t_len)
            last = slice((sub + 1) * t_len - 1, (sub + 1) * t_len)
            c_ref[sub, hd] = decay[last] * c0_ref[sub, hd] + lax.dot_general(
                kw[rows].astype(BF16), v_ref[rows, hd * DV:(hd + 1) * DV], _TN, preferred_element_type=F32)
            n_new = decay[last] * n0_ref[sub, hd:hd + 1, :] + jnp.sum(kw[rows], axis=0, keepdims=True)
            n_out[sub] = jnp.where(head_row == hd, n_new, n_out[sub])
            m_out[sub] = jnp.where(head_lane == hd, m_new[last], m_out[sub])

    for sub in range(n_sub):
        n_ref[sub] = n_out[sub]
        m_ref[sub] = m_out[sub]


N_PROJ_REFS = 5
REC_CHUNK = 256
REC_GROUP = 4


def _mlstm_rec_kernel(*refs, chunk, t_len, n_sub, n_carried):
    p_in, s_in = refs[:N_PROJ_REFS], refs[N_PROJ_REFS:2 * N_PROJ_REFS]
    hn_ref, c0_ref, n0_ref, m0_ref = refs[2 * N_PROJ_REFS:2 * N_PROJ_REFS + 4]
    outs = refs[2 * N_PROJ_REFS + 4 + n_carried:]
    yp_ref, ys_ref = outs[:2]
    _mlstm_prompt_chunk(*p_in, hn_ref, yp_ref, *outs[2:5], chunk=chunk)
    _mlstm_sample_group(*s_in, hn_ref, c0_ref, n0_ref, m0_ref, ys_ref, *outs[5:8], t_len=t_len, n_sub=n_sub)


def _mlstm_rec(j, proj, gates_c, head_norm, state_s, new_p, new_s, dims):
    batch, seq, dec_batch, t_len = dims
    n_tok = proj.shape[0]
    chunk, n_sub = REC_CHUNK, REC_GROUP
    n_chunks = seq // chunk
    assert batch * n_chunks == dec_batch // n_sub
    s_blk = n_sub * t_len
    s_base = batch * seq // s_blk

    def proj_specs(blk, base):
        rows = lambda width_blk: (lambda b, c: (base + b * n_chunks + c, width_blk))
        return [pl.BlockSpec((blk, QK_DIM), rows(0)), pl.BlockSpec((blk, QK_DIM), rows(1)),
                pl.BlockSpec((blk, V_DIM), rows(1)), pl.BlockSpec((blk, V_DIM), rows(2)),
                pl.BlockSpec((blk, LANES), rows(0))]

    def state_specs(n_seq_blk, seq_blk_of):
        return [pl.BlockSpec((None, n_seq_blk, HEADS, DK, DV), lambda b, c: (j, seq_blk_of(b, c), 0, 0, 0)),
                pl.BlockSpec((None, n_seq_blk, HEADS, DK), lambda b, c: (j, seq_blk_of(b, c), 0, 0)),
                pl.BlockSpec((None, n_seq_blk, 1, LANES), lambda b, c: (j, seq_blk_of(b, c), 0, 0))]

    def state_shapes(n_seq):
        return [jax.ShapeDtypeStruct((N_MIX, n_seq, HEADS, DK, DV), F32),
                jax.ShapeDtypeStruct((N_MIX, n_seq, HEADS, DK), F32),
                jax.ShapeDtypeStruct((N_MIX, n_seq, 1, LANES), F32)]

    p_state = state_specs(1, lambda b, c: b)
    s_state = state_specs(n_sub, lambda b, c: b * n_chunks + c)
    carried = {}
    if new_p is not None:
        carried.update({2 + i: a for i, a in enumerate(new_p)})
        carried.update({5 + i: a for i, a in enumerate(new_s)})
    proj_ops = (proj, proj, proj, proj, gates_c)
    outs = _call(
        functools.partial(_mlstm_rec_kernel, chunk=chunk, t_len=t_len, n_sub=n_sub, n_carried=len(carried)),
        name="mlstm_rec", grid=(batch, n_chunks), sem=("arbitrary", "arbitrary"),
        in_specs=proj_specs(chunk, 0) + proj_specs(s_blk, s_base) + [_layer_row(j)] + s_state,
        out_specs=[pl.BlockSpec((chunk, V_DIM), lambda b, c: (b * n_chunks + c, 0)),
                   pl.BlockSpec((s_blk, V_DIM), lambda b, c: (b * n_chunks + c, 0))] + p_state + s_state,
        out_shape=[jax.ShapeDtypeStruct((batch * seq, V_DIM), BF16),
                   jax.ShapeDtypeStruct((dec_batch * t_len, V_DIM), BF16)]
                  + state_shapes(batch) + state_shapes(dec_batch),
        operands=proj_ops + proj_ops + (head_norm,) + tuple(state_s),
        carried=carried)
    return outs[0], outs[1], tuple(outs[2:5]), tuple(outs[5:8])


def _out_proj_kernel(x_ref, yp_ref, ys_ref, w_ref, o_ref, *, n_prompt_tiles):
    y = jnp.where(pl.program_id(0) < n_prompt_tiles, yp_ref[...], ys_ref[...])
    o_ref[...] = x_ref[...] + jnp.dot(y, w_ref[...], preferred_element_type=F32)


def _out_proj(j, h, y_prompt, y_sample, w_bf, n_prompt):
    n_tok = h.shape[0]
    tm = 512
    npt = n_prompt // tm
    row_spec = pl.BlockSpec((tm, D_MODEL), lambda i: (i, 0))
    return _call(
        functools.partial(_out_proj_kernel, n_prompt_tiles=npt),
        name="mlstm_out", grid=(n_tok // tm,), sem=("arbitrary",),
        in_specs=[row_spec,
                  pl.BlockSpec((tm, V_DIM), lambda i: (jnp.minimum(i, npt - 1), 0)),
                  pl.BlockSpec((tm, V_DIM), lambda i: (jnp.maximum(i - npt, 0), 0)),
                  pl.BlockSpec((None, V_DIM, D_MODEL), lambda i: (j, 0, 0))],
        out_specs=row_spec, out_shape=jax.ShapeDtypeStruct((n_tok, D_MODEL), F32),
        operands=(h, y_prompt, y_sample, w_bf))


def _mlstm_layer(j, h, u, state_s, new_p, new_s, w_t, wg_t, b_gate, head_norm, w_out_bf, dims):
    proj, gates_c = _mlstm_in(j, u, w_t, wg_t, b_gate)
    y_prompt, y_sample, new_p, new_s = _mlstm_rec(j, proj, gates_c, head_norm, state_s, new_p, new_s, dims)
    return _out_proj(j, h, y_prompt, y_sample, w_out_bf, dims[0] * dims[1]), new_p, new_s


def kernel(x_prompt, x_sample, state_pool, state_mlstm_C, state_mlstm_n, state_mlstm_m, norm_mix, norm_ffn, norm_final, pool_w, pool_scale, mlstm_w_in, mlstm_b_gate, mlstm_head_norm, mlstm_w_out, ffn_w_up, ffn_w_down):
    batch, seq, _ = x_prompt.shape
    dec_batch, t_len, _ = x_sample.shape
    dims = (batch, seq, dec_batch, t_len)
    n_prompt = batch * seq
    x_prompt = x_prompt.reshape(n_prompt, D_MODEL)
    x_sample = x_sample.reshape(dec_batch * t_len, D_MODEL)

    rows3 = lambda a: a.reshape(a.shape[0], 1, a.shape[1])
    norm_mix, norm_ffn, pool_scale, head_norm = map(rows3, (norm_mix, norm_ffn, pool_scale, mlstm_head_norm))
    pad_lanes = lambda a: jnp.pad(a, [(0, 0)] * (a.ndim - 1) + [(0, LANES - a.shape[-1])])
    b_gate = rows3(pad_lanes(mlstm_b_gate))
    state_s = (state_mlstm_C, state_mlstm_n, pad_lanes(state_mlstm_m)[:, :, None, :])
    pool_w_bf = pool_w.astype(BF16)
    w_in_t = jnp.swapaxes(mlstm_w_in, 1, 2)
    w_gate_t = jnp.pad(w_in_t[:, QKVO_DIM:, :], ((0, 0), (0, LANES - 2 * HEADS), (0, 0)))
    w_out_bf = mlstm_w_out.astype(BF16)
    ffn_bf = (ffn_w_up[0].astype(BF16), ffn_w_down[0].astype(BF16))
    norm_final = norm_final.reshape(1, D_MODEL)

    h = u = new_pool = new_p = new_s = None
    for layer in range(DEPTH):
        j = layer // 2
        if layer % 2 == 0:
            h, new_pool = _pool_layer(j, layer, h, x_prompt, x_sample, state_pool, new_pool,
                                      norm_mix, pool_w_bf, pool_scale, dims)
        else:
            h, new_p, new_s = _mlstm_layer(j, h, u, state_s, new_p, new_s, w_in_t, w_gate_t,
                                           b_gate, head_norm, w_out_bf, dims)
        h, u, ffn_bf = _ffn_layer(layer, h, norm_ffn, *ffn_bf, ffn_w_up, ffn_w_down, norm_mix, norm_final,
                                  n_prompt)

    y_prompt = h[0].reshape(batch, seq, D_MODEL)
    y_sample = h[1].reshape(dec_batch, t_len, D_MODEL)
    return (y_prompt, y_sample, new_pool[0], new_pool[1],
            new_p[0], new_p[1], new_p[2][:, :, 0, :HEADS],
            new_s[0], new_s[1], new_s[2][:, :, 0, :HEADS])
```

```python
import functools

import jax
import jax.numpy as jnp
from jax import lax
from jax.experimental import pallas as pl
from jax.experimental.pallas import tpu as pltpu

F32 = jnp.float32
BF16 = jnp.bfloat16

D_MODEL = 2048
DEPTH = 4
N_MIX = DEPTH // 2
POOL_WINDOWS = (2, 4, 8, 16)
N_GROUPS = len(POOL_WINDOWS)
GROUP_DIM = D_MODEL // N_GROUPS
POOL_BUF = max(POOL_WINDOWS) - 1
assert POOL_WINDOWS[0] == 2 and all(b == 2 * a for a, b in zip(POOL_WINDOWS, POOL_WINDOWS[1:]))
HALO = POOL_BUF + 1
PAD = 8
HEADS = 4
QK_DIM = D_MODEL // 2
V_DIM = D_MODEL
DK = QK_DIM // HEADS
DV = V_DIM // HEADS
QKVO_DIM = 2 * QK_DIM + 2 * V_DIM
GATE_CAP = 15.0
D_FF = 4 * D_MODEL
EPS = 1e-6
LANES = 128

VMEM_LIMIT = 60 * 1024 * 1024

POOL_ROWS = 512
POOL_SEQS = 16
FFN_TM, FFN_FC = 768, 1024
FFN_FINAL_TILE = (512, 1024)
CAST_SPLIT = 8
IN_TM, IN_NB = 1536, 1024
OUT_CAST_STEPS = 32
REC_CHUNK = 256
REC_GROUP = 4
OUT_TM = 512


def _rmsnorm(x, g):
    return x * lax.rsqrt(jnp.mean(x * x, axis=-1, keepdims=True) + EPS) * g


def _call(body, *, name, grid, sem, in_specs, out_specs, out_shape, operands, carried=None, aliases=None,
          scratch_shapes=()):
    carried = carried or {}
    aliases = dict(aliases or {})
    for pos, out_idx in enumerate(carried):
        aliases[len(operands) + pos] = out_idx
    return pl.pallas_call(
        body,
        grid=grid,
        in_specs=list(in_specs) + [pl.BlockSpec(memory_space=pl.ANY)] * len(carried),
        out_specs=out_specs,
        out_shape=out_shape,
        scratch_shapes=list(scratch_shapes),
        input_output_aliases=aliases,
        compiler_params=pltpu.CompilerParams(dimension_semantics=sem, vmem_limit_bytes=VMEM_LIMIT),
        name=name,
    )(*operands, *carried.values())


def _layer_row(layer, d=D_MODEL):
    return pl.BlockSpec((None, 1, d), lambda *_: (layer, 0, 0))


def _pool_prompt_kernel(h_ref, g_ref, w_ref, s_ref, *rest, tt, n_carried):
    o_ref, nb_ref, ext_ref, *lvl_refs = rest[n_carried:]
    t = pl.program_id(1)
    base = PAD + HALO
    end = base + tt

    @pl.when(t == 0)
    def _():
        ext_ref[0:base, :] = jnp.zeros((base, D_MODEL), F32)
        for ref in lvl_refs:
            ref[0:PAD, :] = jnp.zeros((PAD, D_MODEL), F32)

    x = h_ref[...]
    ext_ref[base:end, :] = _rmsnorm(x, g_ref[...])
    n_seen = t * tt + 1 + lax.broadcasted_iota(jnp.int32, (tt, 1), 0)
    src = ext_ref
    outs = []
    for g, w in enumerate(POOL_WINDOWS):
        half = w // 2
        c0 = g * GROUP_DIM
        cs = slice(c0, c0 + GROUP_DIM)
        if g < len(lvl_refs):
            dst = lvl_refs[g]
            dst[PAD:end, c0:] = src[PAD:end, c0:] + src[PAD - half:end - half, c0:]
            s = dst[base:end, cs]
            src = dst
        else:
            s = src[base:end, cs] + src[base - half:end - half, cs]
        cur = ext_ref[base:end, cs]
        d = s / jnp.minimum(n_seen, w).astype(F32) - cur
        outs.append(jnp.dot(d.astype(BF16), w_ref[g], preferred_element_type=F32))
    o_ref[...] = x + jnp.concatenate(outs, axis=-1) * s_ref[...]

    @pl.when(t == pl.num_programs(1) - 1)
    def _():
        nb_ref[0] = ext_ref[end - POOL_BUF:end, :]

    ext_ref[PAD:base, :] = ext_ref[end - HALO:end, :]


def _pool_sample_kernel(h_ref, buf_ref, g_ref, w_ref, s_ref, *rest, sb, t_len, n_carried):
    o_ref, nb_ref, ext_ref = rest[n_carried:]
    x = h_ref[...]
    u = _rmsnorm(x, g_ref[...])
    ext_ref[:, 1:HALO, :] = buf_ref[...]
    ext_ref[:, HALO:HALO + t_len, :] = u.reshape(sb, t_len, D_MODEL)
    outs = []
    for g, w in enumerate(POOL_WINDOWS):
        cs = slice(g * GROUP_DIM, (g + 1) * GROUP_DIM)
        cur = ext_ref[:, HALO:HALO + t_len, cs]
        s = cur
        for i in range(1, w):
            s = s + ext_ref[:, HALO - i:HALO - i + t_len, cs]
        d = (s * (1.0 / w) - cur).reshape(sb * t_len, GROUP_DIM)
        outs.append(jnp.dot(d.astype(BF16), w_ref[g], preferred_element_type=F32))
    o_ref[...] = x + jnp.concatenate(outs, axis=-1) * s_ref[...]
    nb_ref[...] = ext_ref[:, HALO + t_len - POOL_BUF:HALO + t_len, :]


def _pool_layer(j, layer, h, x_prompt, x_sample, state_pool, new_pool, gains, w_bf, scales, dims):
    batch, seq, dec_batch, t_len = dims
    n_prompt, n_tok = batch * seq, batch * seq + dec_batch * t_len
    first = h is None
    h_shape = jax.ShapeDtypeStruct((n_tok, D_MODEL), F32)
    w_spec = pl.BlockSpec((None, N_GROUPS, GROUP_DIM, GROUP_DIM), lambda *_: (j, 0, 0, 0))

    tt = POOL_ROWS
    nt = seq // tt
    row_spec = pl.BlockSpec((tt, D_MODEL), lambda b, t: (b * nt + t, 0))
    h, nb_p = _call(
        functools.partial(_pool_prompt_kernel, tt=tt, n_carried=0 if first else 1),
        name="pool_prompt", grid=(batch, nt), sem=("arbitrary", "arbitrary"),
        in_specs=[row_spec, _layer_row(layer), w_spec, _layer_row(j)],
        out_specs=[row_spec, pl.BlockSpec((None, 1, POOL_BUF, D_MODEL), lambda b, t: (j, b, 0, 0))],
        out_shape=[h_shape, jax.ShapeDtypeStruct((N_MIX, batch, POOL_BUF, D_MODEL), F32)],
        operands=(x_prompt if first else h, gains, w_bf, scales),
        carried=None if first else {1: new_pool[0]},
        aliases=None if first else {0: 0},
        scratch_shapes=[pltpu.VMEM((PAD + HALO + tt, D_MODEL), F32)] * N_GROUPS)

    sb = POOL_SEQS
    rows = sb * t_len
    base = n_prompt // rows
    out_rows = pl.BlockSpec((rows, D_MODEL), lambda i: (base + i, 0))
    h, nb_s = _call(
        functools.partial(_pool_sample_kernel, sb=sb, t_len=t_len, n_carried=1),
        name="pool_sample", grid=(dec_batch // sb,), sem=("arbitrary",),
        in_specs=[pl.BlockSpec((rows, D_MODEL), lambda i: (i, 0)) if first else out_rows,
                  pl.BlockSpec((None, sb, POOL_BUF, D_MODEL), lambda i: (j, i, 0, 0)),
                  _layer_row(layer), w_spec, _layer_row(j)],
        out_specs=[out_rows, pl.BlockSpec((None, sb, POOL_BUF, D_MODEL), lambda i: (j, i, 0, 0))],
        out_shape=[h_shape, jax.ShapeDtypeStruct((N_MIX, dec_batch, POOL_BUF, D_MODEL), F32)],
        operands=(x_sample if first else h, state_pool, gains, w_bf, scales),
        carried={0: h} if first else {1: new_pool[1]},
        aliases=None if first else {0: 0},
        scratch_shapes=[pltpu.VMEM((sb, HALO + t_len, D_MODEL), F32)])
    return h, (nb_p, nb_s)


def _ffn_tile(x_ref, g_ref, wu_ref, wd_ref, o_ref, u_ref):
    @pl.when(pl.program_id(1) == 0)
    def _():
        x = x_ref[...]
        u_ref[...] = _rmsnorm(x, g_ref[...]).astype(BF16)
        o_ref[...] = x

    a = jnp.maximum(jnp.dot(u_ref[...], wu_ref[...], preferred_element_type=F32), 0.0)
    o_ref[...] += jnp.dot((a * a).astype(BF16), wd_ref[...], preferred_element_type=F32)


def _ffn_kernel(x_ref, g_ref, wu_ref, wd_ref, *rest, emit_u):
    rest = list(rest)
    gn_ref = rest.pop(0) if emit_u else None
    wun_ref, wdn_ref, o_ref = rest.pop(0), rest.pop(0), rest.pop(0)
    un_ref = rest.pop(0) if emit_u else None
    wun_bf_ref, wdn_bf_ref, u_ref = rest
    wun_bf_ref[...] = wun_ref[...].astype(BF16)
    wdn_bf_ref[...] = wdn_ref[...].astype(BF16)
    _ffn_tile(x_ref, g_ref, wu_ref, wd_ref, o_ref, u_ref)

    if emit_u:
        @pl.when(pl.program_id(1) == pl.num_programs(1) - 1)
        def _():
            un_ref[...] = _rmsnorm(o_ref[...], gn_ref[...]).astype(BF16)


def _ffn_final_kernel(x_ref, g_ref, wu_ref, wd_ref, gf_ref, yp_ref, ys_ref, u_ref, *, n_prompt_tiles):
    def tile(o_ref):
        _ffn_tile(x_ref, g_ref, wu_ref, wd_ref, o_ref, u_ref)

        @pl.when(pl.program_id(1) == pl.num_programs(1) - 1)
        def _():
            o_ref[...] = _rmsnorm(o_ref[...], gf_ref[...])

    is_prompt = pl.program_id(0) < n_prompt_tiles
    pl.when(is_prompt)(lambda: tile(yp_ref))
    pl.when(jnp.logical_not(is_prompt))(lambda: tile(ys_ref))


def _ffn_layer(layer, h, gains, wu_bf, wd_bf, w_up_f32, w_down_f32, mix_gains, gain_final, n_prompt):
    n_tok = h.shape[0]
    tm, fc = FFN_FINAL_TILE if layer == DEPTH - 1 else (FFN_TM, FFN_FC)
    n_k = D_FF // fc
    row_spec = pl.BlockSpec((tm, D_MODEL), lambda i, k: (i, 0))
    in_specs = [row_spec, _layer_row(layer),
                pl.BlockSpec((D_MODEL, fc), lambda i, k: (0, k)),
                pl.BlockSpec((fc, D_MODEL), lambda i, k: (k, 0))]
    scratch = [pltpu.VMEM((tm, D_MODEL), BF16)]
    grid = (n_tok // tm, n_k)
    h_shape = jax.ShapeDtypeStruct((n_tok, D_MODEL), F32)
    if layer == DEPTH - 1:
        npt = n_prompt // tm
        ys = _call(
            functools.partial(_ffn_final_kernel, n_prompt_tiles=npt),
            name="ffn_final", grid=grid, sem=("arbitrary", "arbitrary"),
            in_specs=in_specs + [pl.BlockSpec((1, D_MODEL), lambda i, k: (0, 0))],
            out_specs=[pl.BlockSpec((tm, D_MODEL), lambda i, k: (jnp.minimum(i, npt - 1), 0)),
                       pl.BlockSpec((tm, D_MODEL), lambda i, k: (jnp.maximum(i - npt, 0), 0))],
            out_shape=[jax.ShapeDtypeStruct((n_prompt, D_MODEL), F32),
                       jax.ShapeDtypeStruct((n_tok - n_prompt, D_MODEL), F32)],
            operands=(h, gains, wu_bf, wd_bf, gain_final), scratch_shapes=scratch)
        return ys, None, None

    emit_u = layer % 2 == 0
    cast_blk = lambda i, k: (jnp.minimum(i, CAST_SPLIT - 1), jnp.where(i < CAST_SPLIT, k, n_k - 1))
    up_blk, down_blk = (D_MODEL // CAST_SPLIT, fc), (D_FF // CAST_SPLIT, D_MODEL // n_k)
    operands, out_specs, out_shape = [h, gains, wu_bf, wd_bf], [row_spec], [h_shape]
    if emit_u:
        in_specs.append(_layer_row(layer + 1))
        operands.append(mix_gains)
        out_specs.append(row_spec)
        out_shape.append(jax.ShapeDtypeStruct((n_tok, D_MODEL), BF16))
    in_specs += [pl.BlockSpec((None,) + up_blk, lambda i, k: (layer + 1,) + cast_blk(i, k)),
                 pl.BlockSpec((None,) + down_blk, lambda i, k: (layer + 1,) + cast_blk(i, k))]
    operands += [w_up_f32, w_down_f32]
    out_specs += [pl.BlockSpec(up_blk, cast_blk), pl.BlockSpec(down_blk, cast_blk)]
    out_shape += [jax.ShapeDtypeStruct((D_MODEL, D_FF), BF16), jax.ShapeDtypeStruct((D_FF, D_MODEL), BF16)]
    outs = _call(functools.partial(_ffn_kernel, emit_u=emit_u),
                 name="ffn", grid=grid, sem=("arbitrary", "arbitrary"),
                 in_specs=in_specs, out_specs=out_specs, out_shape=out_shape,
                 operands=tuple(operands), scratch_shapes=scratch)
    return outs[0], (outs[1] if emit_u else None), tuple(outs[-2:])


_NT = (((1,), (1,)), ((), ()))


def _mlstm_in_kernel(u_ref, w_ref, wg_ref, bg_ref, wo_ref, p_ref, gc_ref, wo_bf_ref, w_scr):
    jj, i = pl.program_id(0), pl.program_id(1)
    wo_bf_ref[...] = wo_ref[...].astype(BF16)

    @pl.when(i == 0)
    def _():
        w_scr[...] = w_ref[...].astype(BF16)

    u = u_ref[...]
    p_ref[...] = lax.dot_general(u, w_scr[...], _NT, preferred_element_type=F32).astype(BF16)

    @pl.when(jj == 0)
    def _():
        pre = lax.dot_general(u, wg_ref[...].astype(BF16), _NT, preferred_element_type=F32) + bg_ref[...]
        capped = GATE_CAP * jnp.tanh(pre / GATE_CAP)
        log_f = jnp.minimum(capped, 0.0) - jnp.log1p(jnp.exp(-jnp.abs(capped)))
        lane = lax.broadcasted_iota(jnp.int32, capped.shape, 1)
        gc_ref[...] = jnp.where(lane < HEADS, capped, log_f)


def _mlstm_in(j, u, w_t, wg_t, b_gate, w_out):
    n_tok = u.shape[0]
    tm, nb = IN_TM, IN_NB
    n_i = n_tok // tm
    assert OUT_CAST_STEPS <= (QKVO_DIM // nb) * n_i
    wo_rows = V_DIM // OUT_CAST_STEPS
    wo_blk = lambda jj, i: (jnp.minimum(jj * n_i + i, OUT_CAST_STEPS - 1), 0)
    return _call(
        _mlstm_in_kernel, name="mlstm_in", grid=(QKVO_DIM // nb, n_i), sem=("arbitrary", "arbitrary"),
        in_specs=[pl.BlockSpec((tm, D_MODEL), lambda jj, i: (i, 0)),
                  pl.BlockSpec((None, nb, D_MODEL), lambda jj, i: (j, jj, 0)),
                  pl.BlockSpec((None, LANES, D_MODEL), lambda jj, i: (j, 0, 0)),
                  _layer_row(j, LANES),
                  pl.BlockSpec((None, wo_rows, D_MODEL), lambda jj, i: (j,) + wo_blk(jj, i))],
        out_specs=[pl.BlockSpec((tm, nb), lambda jj, i: (i, jj)),
                   pl.BlockSpec((tm, LANES), lambda jj, i: (jnp.where(jj == 0, i, n_i - 1), 0)),
                   pl.BlockSpec((wo_rows, D_MODEL), wo_blk)],
        out_shape=[jax.ShapeDtypeStruct((n_tok, QKVO_DIM), BF16),
                   jax.ShapeDtypeStruct((n_tok, LANES), F32),
                   jax.ShapeDtypeStruct((V_DIM, D_MODEL), BF16)],
        operands=(u, w_t, wg_t, b_gate, w_out),
        scratch_shapes=[pltpu.VMEM((nb, D_MODEL), BF16)])


_TN = (((0,), (0,)), ((), ()))


def _gated_heads(num, den, m_row, o, hn):
    inv = 1.0 / jnp.maximum(jnp.abs(den), jnp.exp(-m_row))
    scale = inv * lax.rsqrt(jnp.mean(num * num, axis=-1, keepdims=True) * (inv * inv) + EPS)
    return (num * scale * (hn * jax.nn.sigmoid(o.astype(F32)))).astype(BF16)


def _bf16_terms(x):
    hi = x.astype(BF16)
    rest = x - hi.astype(F32)
    mid = rest.astype(BF16)
    return hi, mid, (rest - mid.astype(F32)).astype(BF16)


def _select_sum(sel, terms, dims, sel_is_lhs):
    sel = sel.astype(BF16)
    dot = lambda t: lax.dot_general(*((sel, t) if sel_is_lhs else (t, sel)), (dims, ((), ())),
                                    preferred_element_type=F32)
    return dot(terms[0]) + dot(terms[1]) + dot(terms[2])


def _gate_sums(gates_c, causal, causal_t):
    pick = (lax.broadcasted_iota(jnp.int32, (2 * HEADS, LANES), 0)
            == lax.broadcasted_iota(jnp.int32, (2 * HEADS, LANES), 1))
    terms_c = _bf16_terms(gates_c)
    gates_r = _select_sum(pick, terms_c, ((1,), (1,)), True)
    cum_c = _select_sum(causal, terms_c, ((1,), (0,)), True)
    cum_r = _select_sum(causal_t, _bf16_terms(gates_r), ((1,), (0,)), False)
    return gates_r, cum_c, cum_r


def _mlstm_prompt_chunk(q_ref, k_ref, v_ref, o_ref, gc_ref, hn_ref, y_ref, c_ref, n_ref, m_ref, *, chunk):
    @pl.when(pl.program_id(1) == 0)
    def _():
        c_ref[...] = jnp.zeros(c_ref.shape, F32)
        n_ref[...] = jnp.zeros(n_ref.shape, F32)
        m_ref[...] = jnp.zeros(m_ref.shape, F32)

    head_row = lax.broadcasted_iota(jnp.int32, (HEADS, DK), 0)
    head_lane = lax.broadcasted_iota(jnp.int32, (1, LANES), 1)
    row = lax.broadcasted_iota(jnp.int32, (chunk, chunk), 0)
    col = lax.broadcasted_iota(jnp.int32, (chunk, chunk), 1)
    causal = col <= row
    gates_c = gc_ref[...]
    gates_r, cum_c, cum_r = _gate_sums(gates_c, causal, row <= col)
    n_all = n_ref[0]
    m_all = m_ref[0]
    n_out = jnp.zeros((HEADS, DK), F32)
    m_out = jnp.zeros((1, LANES), F32)

    for hd in range(HEADS):
        q = q_ref[:, hd * DK:(hd + 1) * DK] * BF16(DK ** -0.5)
        k = k_ref[:, hd * DK:(hd + 1) * DK]
        v = v_ref[:, hd * DV:(hd + 1) * DV]
        i_col = gates_c[:, hd:hd + 1]
        i_row = gates_r[hd:hd + 1, :]
        b_col = cum_c[:, HEADS + hd:HEADS + hd + 1]
        b_row = cum_r[HEADS + hd:HEADS + hd + 1, :]
        c_prev = c_ref[0, hd]
        m_prev = m_all[:, hd:hd + 1]

        a = b_col + m_prev
        dm = jnp.where(causal, b_col - b_row + i_row, -jnp.inf)
        m_row = jnp.maximum(a, jnp.max(dm, axis=1, keepdims=True))
        w_intra = jnp.exp(dm - m_row)
        w_inter = jnp.exp(a - m_row)
        s = lax.dot_general(q, k, _NT, preferred_element_type=F32) * w_intra
        num = (w_inter * jnp.dot(q, c_prev.astype(BF16), preferred_element_type=F32)
               + jnp.dot(s.astype(BF16), v, preferred_element_type=F32))
        den = (w_inter * jnp.sum(q.astype(F32) * n_all[hd:hd + 1, :], axis=1, keepdims=True)
               + jnp.sum(s, axis=1, keepdims=True))

        b_last = b_col[chunk - 1:chunk, :]
        m_new = m_row[chunk - 1:chunk, :]
        decay = jnp.exp(b_last + m_prev - m_new)
        kw = k.astype(F32) * jnp.exp(b_last - b_col + i_col - m_new)
        c_ref[0, hd] = decay * c_prev + lax.dot_general(kw.astype(BF16), v, _TN, preferred_element_type=F32)
        n_new = decay * n_all[hd:hd + 1, :] + jnp.sum(kw, axis=0, keepdims=True)
        n_out = jnp.where(head_row == hd, n_new, n_out)
        m_out = jnp.where(head_lane == hd, m_new, m_out)

        y_ref[:, hd * DV:(hd + 1) * DV] = _gated_heads(
            num, den, m_row, o_ref[:, hd * DV:(hd + 1) * DV], hn_ref[:, hd * DV:(hd + 1) * DV])

    n_ref[0] = n_out
    m_ref[0] = m_out


def _mlstm_sample_group(q_ref, k_ref, v_ref, o_ref, gc_ref, hn_ref, c0_ref, n0_ref, m0_ref,
                        y_ref, c_ref, n_ref, m_ref, *, t_len, n_sub):
    n_rows = n_sub * t_len
    row = lax.broadcasted_iota(jnp.int32, (n_rows, n_rows), 0)
    col = lax.broadcasted_iota(jnp.int32, (n_rows, n_rows), 1)
    seq_start = lax.broadcasted_iota(jnp.int32, (n_sub, t_len, n_rows), 0).reshape(n_rows, n_rows) * t_len
    same_seq = (col >= seq_start) & (col < seq_start + t_len)
    causal = same_seq & (col <= row)
    causal_t = same_seq & (row <= col)

    def per_seq(x):
        return jnp.broadcast_to(x, (n_sub, t_len, x.shape[-1])).reshape(n_rows, x.shape[-1])

    def last_token(x):
        return per_seq(x.reshape(n_sub, t_len, x.shape[-1])[:, t_len - 1:t_len, :])

    gates_c = gc_ref[...]
    gates_r, cum_c, cum_r = _gate_sums(gates_c, causal, causal_t)
    m_prev_all = per_seq(m0_ref[...])
    b_last_all = last_token(cum_c)
    head_row = lax.broadcasted_iota(jnp.int32, (HEADS, DK), 0)
    head_lane = lax.broadcasted_iota(jnp.int32, (1, LANES), 1)
    n_out = [jnp.zeros((HEADS, DK), F32)] * n_sub
    m_out = [jnp.zeros((1, LANES), F32)] * n_sub

    for hd in range(HEADS):
        q = q_ref[:, hd * DK:(hd + 1) * DK] * BF16(DK ** -0.5)
        k = k_ref[:, hd * DK:(hd + 1) * DK]
        v = v_ref[:, hd * DV:(hd + 1) * DV]
        i_col = gates_c[:, hd:hd + 1]
        i_row = gates_r[hd:hd + 1, :]
        b_col = cum_c[:, HEADS + hd:HEADS + hd + 1]
        b_row = cum_r[HEADS + hd:HEADS + hd + 1, :]
        m_prev = m_prev_all[:, hd:hd + 1]
        b_last = b_last_all[:, HEADS + hd:HEADS + hd + 1]

        a = b_col + m_prev
        dm = jnp.where(causal, b_col - b_row + i_row, -jnp.inf)
        m_row = jnp.maximum(a, jnp.max(dm, axis=1, keepdims=True))
        w_intra = jnp.exp(dm - m_row)
        w_inter = jnp.exp(a - m_row)
        s = lax.dot_general(q, k, _NT, preferred_element_type=F32) * w_intra
        inter = jnp.concatenate(
            [jnp.dot(q_ref[sub * t_len:(sub + 1) * t_len, hd * DK:(hd + 1) * DK] * BF16(DK ** -0.5),
                     c0_ref[sub, hd].astype(BF16), preferred_element_type=F32)
             for sub in range(n_sub)], axis=0)
        num = w_inter * inter + jnp.dot(s.astype(BF16), v, preferred_element_type=F32)
        n_prev = per_seq(n0_ref[:, hd:hd + 1, :])
        den = (w_inter * jnp.sum(q.astype(F32) * n_prev, axis=1, keepdims=True)
               + jnp.sum(s, axis=1, keepdims=True))
        y_ref[:, hd * DV:(hd + 1) * DV] = _gated_heads(
            num, den, m_row, o_ref[:, hd * DV:(hd + 1) * DV], hn_ref[:, hd * DV:(hd + 1) * DV])

        m_new = last_token(m_row)
        decay = jnp.exp(b_last + m_prev - m_new)
        kw = k.astype(F32) * jnp.exp(b_last - b_col + i_col - m_new)
        for sub in range(n_sub):
            rows = slice(sub * t_len, (sub + 1) * t_len)
            last = slice((sub + 1) * t_len - 1, (sub + 1) * t_len)
            c_ref[sub, hd] = decay[last] * c0_ref[sub, hd] + lax.dot_general(
                kw[rows].astype(BF16), v_ref[rows, hd * DV:(hd + 1) * DV], _TN, preferred_element_type=F32)
            n_new = decay[last] * n0_ref[sub, hd:hd + 1, :] + jnp.sum(kw[rows], axis=0, keepdims=True)
            n_out[sub] = jnp.where(head_row == hd, n_new, n_out[sub])
            m_out[sub] = jnp.where(head_lane == hd, m_new[last], m_out[sub])

    for sub in range(n_sub):
        n_ref[sub] = n_out[sub]
        m_ref[sub] = m_out[sub]


N_PROJ_REFS = 5


def _mlstm_rec_kernel(*refs, chunk, t_len, n_sub, n_carried):
    p_in, s_in = refs[:N_PROJ_REFS], refs[N_PROJ_REFS:2 * N_PROJ_REFS]
    hn_ref, c0_ref, n0_ref, m0_ref = refs[2 * N_PROJ_REFS:2 * N_PROJ_REFS + 4]
    outs = refs[2 * N_PROJ_REFS + 4 + n_carried:]
    yp_ref, ys_ref = outs[:2]
    _mlstm_prompt_chunk(*p_in, hn_ref, yp_ref, *outs[2:5], chunk=chunk)
    _mlstm_sample_group(*s_in, hn_ref, c0_ref, n0_ref, m0_ref, ys_ref, *outs[5:8], t_len=t_len, n_sub=n_sub)


def _mlstm_rec(j, proj, gates_c, head_norm, state_s, new_p, new_s, dims):
    batch, seq, dec_batch, t_len = dims
    chunk, n_sub = REC_CHUNK, REC_GROUP
    n_chunks = seq // chunk
    assert batch * n_chunks == dec_batch // n_sub
    s_blk = n_sub * t_len
    s_base = batch * seq // s_blk

    def proj_specs(blk, base):
        rows = lambda width_blk: (lambda b, c: (base + b * n_chunks + c, width_blk))
        return [pl.BlockSpec((blk, QK_DIM), rows(0)), pl.BlockSpec((blk, QK_DIM), rows(1)),
                pl.BlockSpec((blk, V_DIM), rows(1)), pl.BlockSpec((blk, V_DIM), rows(2)),
                pl.BlockSpec((blk, LANES), rows(0))]

    def state_specs(n_seq_blk, seq_blk_of):
        return [pl.BlockSpec((None, n_seq_blk, HEADS, DK, DV), lambda b, c: (j, seq_blk_of(b, c), 0, 0, 0)),
                pl.BlockSpec((None, n_seq_blk, HEADS, DK), lambda b, c: (j, seq_blk_of(b, c), 0, 0)),
                pl.BlockSpec((None, n_seq_blk, 1, LANES), lambda b, c: (j, seq_blk_of(b, c), 0, 0))]

    def state_shapes(n_seq):
        return [jax.ShapeDtypeStruct((N_MIX, n_seq, HEADS, DK, DV), F32),
                jax.ShapeDtypeStruct((N_MIX, n_seq, HEADS, DK), F32),
                jax.ShapeDtypeStruct((N_MIX, n_seq, 1, LANES), F32)]

    p_state = state_specs(1, lambda b, c: b)
    s_state = state_specs(n_sub, lambda b, c: b * n_chunks + c)
    carried = {}
    if new_p is not None:
        carried.update({2 + i: a for i, a in enumerate(new_p)})
        carried.update({5 + i: a for i, a in enumerate(new_s)})
    proj_ops = (proj, proj, proj, proj, gates_c)
    outs = _call(
        functools.partial(_mlstm_rec_kernel, chunk=chunk, t_len=t_len, n_sub=n_sub, n_carried=len(carried)),
        name="mlstm_rec", grid=(batch, n_chunks), sem=("arbitrary", "arbitrary"),
        in_specs=proj_specs(chunk, 0) + proj_specs(s_blk, s_base) + [_layer_row(j)] + s_state,
        out_specs=[pl.BlockSpec((chunk, V_DIM), lambda b, c: (b * n_chunks + c, 0)),
                   pl.BlockSpec((s_blk, V_DIM), lambda b, c: (b * n_chunks + c, 0))] + p_state + s_state,
        out_shape=[jax.ShapeDtypeStruct((batch * seq, V_DIM), BF16),
                   jax.ShapeDtypeStruct((dec_batch * t_len, V_DIM), BF16)]
                  + state_shapes(batch) + state_shapes(dec_batch),
        operands=proj_ops + proj_ops + (head_norm,) + tuple(state_s),
        carried=carried)
    return outs[0], outs[1], tuple(outs[2:5]), tuple(outs[5:8])


def _out_proj_kernel(x_ref, yp_ref, ys_ref, w_ref, o_ref, *, n_prompt_tiles):
    y = jnp.where(pl.program_id(0) < n_prompt_tiles, yp_ref[...], ys_ref[...])
    o_ref[...] = x_ref[...] + jnp.dot(y, w_ref[...], preferred_element_type=F32)


def _out_proj(h, y_prompt, y_sample, w_bf, n_prompt):
    n_tok = h.shape[0]
    tm = OUT_TM
    npt = n_prompt // tm
    row_spec = pl.BlockSpec((tm, D_MODEL), lambda i: (i, 0))
    return _call(
        functools.partial(_out_proj_kernel, n_prompt_tiles=npt),
        name="mlstm_out", grid=(n_tok // tm,), sem=("arbitrary",),
        in_specs=[row_spec,
                  pl.BlockSpec((tm, V_DIM), lambda i: (jnp.minimum(i, npt - 1), 0)),
                  pl.BlockSpec((tm, V_DIM), lambda i: (jnp.maximum(i - npt, 0), 0)),
                  pl.BlockSpec((V_DIM, D_MODEL), lambda i: (0, 0))],
        out_specs=row_spec, out_shape=jax.ShapeDtypeStruct((n_tok, D_MODEL), F32),
        operands=(h, y_prompt, y_sample, w_bf))


def _mlstm_layer(j, h, u, state_s, new_p, new_s, w_t, wg_t, b_gate, head_norm, w_out, dims):
    proj, gates_c, w_out_bf = _mlstm_in(j, u, w_t, wg_t, b_gate, w_out)
    y_prompt, y_sample, new_p, new_s = _mlstm_rec(j, proj, gates_c, head_norm, state_s, new_p, new_s, dims)
    return _out_proj(h, y_prompt, y_sample, w_out_bf, dims[0] * dims[1]), new_p, new_s


def kernel(x_prompt, x_sample, state_pool, state_mlstm_C, state_mlstm_n, state_mlstm_m, norm_mix, norm_ffn, norm_final, pool_w, pool_scale, mlstm_w_in, mlstm_b_gate, mlstm_head_norm, mlstm_w_out, ffn_w_up, ffn_w_down):
    batch, seq, _ = x_prompt.shape
    dec_batch, t_len, _ = x_sample.shape
    dims = (batch, seq, dec_batch, t_len)
    n_prompt = batch * seq
    x_prompt = x_prompt.reshape(n_prompt, D_MODEL)
    x_sample = x_sample.reshape(dec_batch * t_len, D_MODEL)

    rows3 = lambda a: a.reshape(a.shape[0], 1, a.shape[1])
    norm_mix, norm_ffn, pool_scale, head_norm = map(rows3, (norm_mix, norm_ffn, pool_scale, mlstm_head_norm))
    pad_lanes = lambda a: jnp.pad(a, [(0, 0)] * (a.ndim - 1) + [(0, LANES - a.shape[-1])])
    b_gate = rows3(pad_lanes(mlstm_b_gate))
    state_s = (state_mlstm_C, state_mlstm_n, pad_lanes(state_mlstm_m)[:, :, None, :])
    pool_w_bf = pool_w.astype(BF16)
    w_in_t = jnp.swapaxes(mlstm_w_in, 1, 2)
    w_gate_t = jnp.pad(w_in_t[:, QKVO_DIM:, :], ((0, 0), (0, LANES - 2 * HEADS), (0, 0)))
    ffn_bf =(ffn_w_up[0].astype(BF16), ffn_w_down[0].astype(BF16))
    norm_final = norm_final.reshape(1, D_MODEL)

    h = u = new_pool = new_p = new_s = None
    for layer in range(DEPTH):
        j = layer // 2
        if layer % 2 == 0:
            h, new_pool = _pool_layer(j, layer, h, x_prompt, x_sample, state_pool, new_pool,
                                      norm_mix, pool_w_bf, pool_scale, dims)
        else:
            h, new_p, new_s = _mlstm_layer(j, h, u, state_s, new_p, new_s, w_in_t, w_gate_t,
                                           b_gate, head_norm, mlstm_w_out, dims)
        h, u, ffn_bf = _ffn_layer(layer, h, norm_ffn, *ffn_bf, ffn_w_up, ffn_w_down, norm_mix, norm_final,
                                  n_prompt)

    y_prompt = h[0].reshape(batch, seq, D_MODEL)
    y_sample = h[1].reshape(dec_batch, t_len, D_MODEL)
    return (y_prompt, y_sample, new_pool[0], new_pool[1],
            new_p[0], new_p[1], new_p[2][:, :, 0, :HEADS],
            new_s[0], new_s[1], new_s[2][:, :, 0, :HEADS])
```

```python
import functools

import jax
import jax.numpy as jnp
from jax import lax
from jax.experimental import pallas as pl
from jax.experimental.pallas import tpu as pltpu

F32 = jnp.float32
BF16 = jnp.bfloat16

D_MODEL = 2048
DEPTH = 4
N_MIX = DEPTH // 2
POOL_WINDOWS = (2, 4, 8, 16)
N_GROUPS = len(POOL_WINDOWS)
GROUP_DIM = D_MODEL // N_GROUPS
POOL_BUF = max(POOL_WINDOWS) - 1
assert POOL_WINDOWS[0] == 2 and all(b == 2 * a for a, b in zip(POOL_WINDOWS, POOL_WINDOWS[1:]))
HALO = POOL_BUF + 1
PAD = 8
HEADS = 4
QK_DIM = D_MODEL // 2
V_DIM = D_MODEL
DK = QK_DIM // HEADS
DV = V_DIM // HEADS
QKVO_DIM = 2 * QK_DIM + 2 * V_DIM
GATE_CAP = 15.0
D_FF = 4 * D_MODEL
EPS = 1e-6
LANES = 128

VMEM_LIMIT = 60 * 1024 * 1024

POOL_ROWS = 512
POOL_SEQS = 16
FFN_TM, FFN_FC = 768, 1024
FFN_FINAL_TILE = (512, 1024)
CAST_SPLIT = 8
IN_TM, IN_NB = 1536, 1024
OUT_CAST_STEPS = 32
REC_CHUNK = 256
REC_GROUP = 4
OUT_TM = 512


def _rmsnorm(x, g):
    return x * lax.rsqrt(jnp.mean(x * x, axis=-1, keepdims=True) + EPS) * g


def _call(body, *, name, grid, sem, in_specs, out_specs, out_shape, operands, carried=None, aliases=None,
          scratch_shapes=()):
    carried = carried or {}
    aliases = dict(aliases or {})
    for pos, out_idx in enumerate(carried):
        aliases[len(operands) + pos] = out_idx
    return pl.pallas_call(
        body,
        grid=grid,
        in_specs=list(in_specs) + [pl.BlockSpec(memory_space=pl.ANY)] * len(carried),
        out_specs=out_specs,
        out_shape=out_shape,
        scratch_shapes=list(scratch_shapes),
        input_output_aliases=aliases,
        compiler_params=pltpu.CompilerParams(dimension_semantics=sem, vmem_limit_bytes=VMEM_LIMIT),
        name=name,
    )(*operands, *carried.values())


def _layer_row(layer, d=D_MODEL):
    return pl.BlockSpec((None, 1, d), lambda *_: (layer, 0, 0))


def _cast_side_job(rest, n_carried, cast):
    rest = list(rest)
    src = rest.pop(0) if cast else None
    o_ref, nb_ref, *scratch = rest[n_carried:]
    if cast:
        dst = scratch.pop(0)
        dst[...] = src[...].astype(BF16)
    return (o_ref, nb_ref, *scratch)


def _pool_prompt_kernel(h_ref, g_ref, w_ref, s_ref, *rest, tt, n_carried, cast):
    o_ref, nb_ref, ext_ref, *lvl_refs = _cast_side_job(rest, n_carried, cast)
    t = pl.program_id(1)
    base = PAD + HALO
    end = base + tt

    @pl.when(t == 0)
    def _():
        ext_ref[0:base, :] = jnp.zeros((base, D_MODEL), F32)
        for ref in lvl_refs:
            ref[0:PAD, :] = jnp.zeros((PAD, D_MODEL), F32)

    x = h_ref[...]
    ext_ref[base:end, :] = _rmsnorm(x, g_ref[...])
    n_seen = t * tt + 1 + lax.broadcasted_iota(jnp.int32, (tt, 1), 0)
    src = ext_ref
    outs = []
    for g, w in enumerate(POOL_WINDOWS):
        half = w // 2
        c0 = g * GROUP_DIM
        cs = slice(c0, c0 + GROUP_DIM)
        if g < len(lvl_refs):
            dst = lvl_refs[g]
            dst[PAD:end, c0:] = src[PAD:end, c0:] + src[PAD - half:end - half, c0:]
            s = dst[base:end, cs]
            src = dst
        else:
            s = src[base:end, cs] + src[base - half:end - half, cs]
        cur = ext_ref[base:end, cs]
        d = s / jnp.minimum(n_seen, w).astype(F32) - cur
        outs.append(jnp.dot(d.astype(BF16), w_ref[g], preferred_element_type=F32))
    o_ref[...] = x + jnp.concatenate(outs, axis=-1) * s_ref[...]

    @pl.when(t == pl.num_programs(1) - 1)
    def _():
        nb_ref[0] = ext_ref[end - POOL_BUF:end, :]

    ext_ref[PAD:base, :] = ext_ref[end - HALO:end, :]


def _pool_sample_kernel(h_ref, buf_ref, g_ref, w_ref, s_ref, *rest, sb, t_len, n_carried, cast):
    o_ref, nb_ref, ext_ref = _cast_side_job(rest, n_carried, cast)
    x = h_ref[...]
    u = _rmsnorm(x, g_ref[...])
    ext_ref[:, 1:HALO, :] = buf_ref[...]
    ext_ref[:, HALO:HALO + t_len, :] = u.reshape(sb, t_len, D_MODEL)
    outs = []
    for g, w in enumerate(POOL_WINDOWS):
        cs = slice(g * GROUP_DIM, (g + 1) * GROUP_DIM)
        cur = ext_ref[:, HALO:HALO + t_len, cs]
        s = cur
        for i in range(1, w):
            s = s + ext_ref[:, HALO - i:HALO - i + t_len, cs]
        d = (s * (1.0 / w) - cur).reshape(sb * t_len, GROUP_DIM)
        outs.append(jnp.dot(d.astype(BF16), w_ref[g], preferred_element_type=F32))
    o_ref[...] = x + jnp.concatenate(outs, axis=-1) * s_ref[...]
    nb_ref[...] = ext_ref[:, HALO + t_len - POOL_BUF:HALO + t_len, :]


def _pool_layer(j, layer, h, x_prompt, x_sample, state_pool, new_pool, gains, w_bf, scales, w_up, w_down, dims):
    batch, seq, dec_batch, t_len = dims
    n_prompt, n_tok = batch * seq, batch * seq + dec_batch * t_len
    first = h is None
    h_shape = jax.ShapeDtypeStruct((n_tok, D_MODEL), F32)
    w_spec = pl.BlockSpec((None, N_GROUPS, GROUP_DIM, GROUP_DIM), lambda *_: (j, 0, 0, 0))

    tt = POOL_ROWS
    nt = seq // tt
    row_spec = pl.BlockSpec((tt, D_MODEL), lambda b, t: (b * nt + t, 0))
    up_rows = D_MODEL // (batch * nt)
    outs = _call(
        functools.partial(_pool_prompt_kernel, tt=tt, n_carried=0 if first else 1, cast=first),
        name="pool_prompt", grid=(batch, nt), sem=("arbitrary", "arbitrary"),
        in_specs=[row_spec, _layer_row(layer), w_spec, _layer_row(j)]
                 + ([pl.BlockSpec((None, up_rows, D_FF), lambda b, t: (0, b * nt + t, 0))] if first else []),
        out_specs=[row_spec, pl.BlockSpec((None, 1, POOL_BUF, D_MODEL), lambda b, t: (j, b, 0, 0))]
                  + ([pl.BlockSpec((up_rows, D_FF), lambda b, t: (b * nt + t, 0))] if first else []),
        out_shape=[h_shape, jax.ShapeDtypeStruct((N_MIX, batch, POOL_BUF, D_MODEL), F32)]
                  + ([jax.ShapeDtypeStruct((D_MODEL, D_FF), BF16)] if first else []),
        operands=(x_prompt if first else h, gains, w_bf, scales) + ((w_up,) if first else ()),
        carried=None if first else {1: new_pool[0]},
        aliases=None if first else {0: 0},
        scratch_shapes=[pltpu.VMEM((PAD + HALO + tt, D_MODEL), F32)] * N_GROUPS)
    h, nb_p = outs[:2]
    wu_bf = outs[2] if first else None

    sb = POOL_SEQS
    rows = sb * t_len
    base = n_prompt // rows
    out_rows = pl.BlockSpec((rows, D_MODEL), lambda i: (base + i, 0))
    down_rows = D_FF // (dec_batch // sb)
    outs = _call(
        functools.partial(_pool_sample_kernel, sb=sb, t_len=t_len, n_carried=1, cast=first),
        name="pool_sample", grid=(dec_batch // sb,), sem=("arbitrary",),
        in_specs=[pl.BlockSpec((rows, D_MODEL), lambda i: (i, 0)) if first else out_rows,
                  pl.BlockSpec((None, sb, POOL_BUF, D_MODEL), lambda i: (j, i, 0, 0)),
                  _layer_row(layer), w_spec, _layer_row(j)]
                 + ([pl.BlockSpec((None, down_rows, D_MODEL), lambda i: (0, i, 0))] if first else []),
        out_specs=[out_rows, pl.BlockSpec((None, sb, POOL_BUF, D_MODEL), lambda i: (j, i, 0, 0))]
                  + ([pl.BlockSpec((down_rows, D_MODEL), lambda i: (i, 0))] if first else []),
        out_shape=[h_shape, jax.ShapeDtypeStruct((N_MIX, dec_batch, POOL_BUF, D_MODEL), F32)]
                  + ([jax.ShapeDtypeStruct((D_FF, D_MODEL), BF16)] if first else []),
        operands=(x_sample if first else h, state_pool, gains, w_bf, scales) + ((w_down,) if first else ()),
        carried={0: h} if first else {1: new_pool[1]},
        aliases=None if first else {0: 0},
        scratch_shapes=[pltpu.VMEM((sb, HALO + t_len, D_MODEL), F32)])
    h, nb_s = outs[:2]
    return h, (nb_p, nb_s), ((wu_bf, outs[2]) if first else None)


def _ffn_tile(x_ref, g_ref, wu_ref, wd_ref, o_ref, u_ref):
    @pl.when(pl.program_id(1) == 0)
    def _():
        x = x_ref[...]
        u_ref[...] = _rmsnorm(x, g_ref[...]).astype(BF16)
        o_ref[...] = x

    a = jnp.maximum(jnp.dot(u_ref[...], wu_ref[...], preferred_element_type=F32), 0.0)
    o_ref[...] += jnp.dot((a * a).astype(BF16), wd_ref[...], preferred_element_type=F32)


def _ffn_kernel(x_ref, g_ref, wu_ref, wd_ref, *rest, emit_u):
    rest = list(rest)
    gn_ref = rest.pop(0) if emit_u else None
    wun_ref, wdn_ref, o_ref = rest.pop(0), rest.pop(0), rest.pop(0)
    un_ref = rest.pop(0) if emit_u else None
    wun_bf_ref, wdn_bf_ref, u_ref = rest
    wun_bf_ref[...] = wun_ref[...].astype(BF16)
    wdn_bf_ref[...] = wdn_ref[...].astype(BF16)
    _ffn_tile(x_ref, g_ref, wu_ref, wd_ref, o_ref, u_ref)

    if emit_u:
        @pl.when(pl.program_id(1) == pl.num_programs(1) - 1)
        def _():
            un_ref[...] = _rmsnorm(o_ref[...], gn_ref[...]).astype(BF16)


def _ffn_final_kernel(x_ref, g_ref, wu_ref, wd_ref, gf_ref, yp_ref, ys_ref, u_ref, *, n_prompt_tiles):
    def tile(o_ref):
        _ffn_tile(x_ref, g_ref, wu_ref, wd_ref, o_ref, u_ref)

        @pl.when(pl.program_id(1) == pl.num_programs(1) - 1)
        def _():
            o_ref[...] = _rmsnorm(o_ref[...], gf_ref[...])

    is_prompt = pl.program_id(0) < n_prompt_tiles
    pl.when(is_prompt)(lambda: tile(yp_ref))
    pl.when(jnp.logical_not(is_prompt))(lambda: tile(ys_ref))


def _ffn_layer(layer, h, gains, wu_bf, wd_bf, w_up_f32, w_down_f32, mix_gains, gain_final, n_prompt):
    n_tok = h.shape[0]
    tm, fc = FFN_FINAL_TILE if layer == DEPTH - 1 else (FFN_TM, FFN_FC)
    n_k = D_FF // fc
    row_spec = pl.BlockSpec((tm, D_MODEL), lambda i, k: (i, 0))
    in_specs = [row_spec, _layer_row(layer),
                pl.BlockSpec((D_MODEL, fc), lambda i, k: (0, k)),
                pl.BlockSpec((fc, D_MODEL), lambda i, k: (k, 0))]
    scratch = [pltpu.VMEM((tm, D_MODEL), BF16)]
    grid = (n_tok // tm, n_k)
    h_shape = jax.ShapeDtypeStruct((n_tok, D_MODEL), F32)
    if layer == DEPTH - 1:
        npt = n_prompt // tm
        ys = _call(
            functools.partial(_ffn_final_kernel, n_prompt_tiles=npt),
            name="ffn_final", grid=grid, sem=("arbitrary", "arbitrary"),
            in_specs=in_specs + [pl.BlockSpec((1, D_MODEL), lambda i, k: (0, 0))],
            out_specs=[pl.BlockSpec((tm, D_MODEL), lambda i, k: (jnp.minimum(i, npt - 1), 0)),
                       pl.BlockSpec((tm, D_MODEL), lambda i, k: (jnp.maximum(i - npt, 0), 0))],
            out_shape=[jax.ShapeDtypeStruct((n_prompt, D_MODEL), F32),
                       jax.ShapeDtypeStruct((n_tok - n_prompt, D_MODEL), F32)],
            operands=(h, gains, wu_bf, wd_bf, gain_final), scratch_shapes=scratch)
        return ys, None, None

    emit_u = layer % 2 == 0
    cast_blk = lambda i, k: (jnp.minimum(i, CAST_SPLIT - 1), jnp.where(i < CAST_SPLIT, k, n_k - 1))
    up_blk, down_blk = (D_MODEL // CAST_SPLIT, fc), (D_FF // CAST_SPLIT, D_MODEL // n_k)
    operands, out_specs, out_shape = [h, gains, wu_bf, wd_bf], [row_spec], [h_shape]
    if emit_u:
        in_specs.append(_layer_row(layer + 1))
        operands.append(mix_gains)
        out_specs.append(row_spec)
        out_shape.append(jax.ShapeDtypeStruct((n_tok, D_MODEL), BF16))
    in_specs += [pl.BlockSpec((None,) + up_blk, lambda i, k: (layer + 1,) + cast_blk(i, k)),
                 pl.BlockSpec((None,) + down_blk, lambda i, k: (layer + 1,) + cast_blk(i, k))]
    operands += [w_up_f32, w_down_f32]
    out_specs += [pl.BlockSpec(up_blk, cast_blk), pl.BlockSpec(down_blk, cast_blk)]
    out_shape += [jax.ShapeDtypeStruct((D_MODEL, D_FF), BF16), jax.ShapeDtypeStruct((D_FF, D_MODEL), BF16)]
    outs = _call(functools.partial(_ffn_kernel, emit_u=emit_u),
                 name="ffn", grid=grid, sem=("arbitrary", "arbitrary"),
                 in_specs=in_specs, out_specs=out_specs, out_shape=out_shape,
                 operands=tuple(operands), scratch_shapes=scratch)
    return outs[0], (outs[1] if emit_u else None), tuple(outs[-2:])


_NT = (((1,), (1,)), ((), ()))


def _mlstm_in_kernel(u_ref, w_ref, wg_ref, bg_ref, wo_ref, p_ref, gc_ref, wo_bf_ref, w_scr):
    jj, i = pl.program_id(0), pl.program_id(1)
    wo_bf_ref[...] = wo_ref[...].astype(BF16)

    @pl.when(i == 0)
    def _():
        w_scr[...] = w_ref[...].astype(BF16)

    u = u_ref[...]
    p_ref[...] = lax.dot_general(u, w_scr[...], _NT, preferred_element_type=F32).astype(BF16)

    @pl.when(jj == 0)
    def _():
        pre = lax.dot_general(u, wg_ref[...].astype(BF16), _NT, preferred_element_type=F32) + bg_ref[...]
        capped = GATE_CAP * jnp.tanh(pre / GATE_CAP)
        log_f = jnp.minimum(capped, 0.0) - jnp.log1p(jnp.exp(-jnp.abs(capped)))
        lane = lax.broadcasted_iota(jnp.int32, capped.shape, 1)
        gc_ref[...] = jnp.where(lane < HEADS, capped, log_f)


def _mlstm_in(j, u, w_t, wg_t, b_gate, w_out):
    n_tok = u.shape[0]
    tm, nb = IN_TM, IN_NB
    n_i = n_tok // tm
    assert OUT_CAST_STEPS <= (QKVO_DIM // nb) * n_i
    wo_rows = V_DIM // OUT_CAST_STEPS
    wo_blk = lambda jj, i: (jnp.minimum(jj * n_i + i, OUT_CAST_STEPS - 1), 0)
    return _call(
        _mlstm_in_kernel, name="mlstm_in", grid=(QKVO_DIM // nb, n_i), sem=("arbitrary", "arbitrary"),
        in_specs=[pl.BlockSpec((tm, D_MODEL), lambda jj, i: (i, 0)),
                  pl.BlockSpec((None, nb, D_MODEL), lambda jj, i: (j, jj, 0)),
                  pl.BlockSpec((None, LANES, D_MODEL), lambda jj, i: (j, 0, 0)),
                  _layer_row(j, LANES),
                  pl.BlockSpec((None, wo_rows, D_MODEL), lambda jj, i: (j,) + wo_blk(jj, i))],
        out_specs=[pl.BlockSpec((tm, nb), lambda jj, i: (i, jj)),
                   pl.BlockSpec((tm, LANES), lambda jj, i: (jnp.where(jj == 0, i, n_i - 1), 0)),
                   pl.BlockSpec((wo_rows, D_MODEL), wo_blk)],
        out_shape=[jax.ShapeDtypeStruct((n_tok, QKVO_DIM), BF16),
                   jax.ShapeDtypeStruct((n_tok, LANES), F32),
                   jax.ShapeDtypeStruct((V_DIM, D_MODEL), BF16)],
        operands=(u, w_t, wg_t, b_gate, w_out),
        scratch_shapes=[pltpu.VMEM((nb, D_MODEL), BF16)])


_TN = (((0,), (0,)), ((), ()))


def _gated_heads(num, den, m_row, o, hn):
    inv = 1.0 / jnp.maximum(jnp.abs(den), jnp.exp(-m_row))
    scale = inv * lax.rsqrt(jnp.mean(num * num, axis=-1, keepdims=True) * (inv * inv) + EPS)
    return (num * scale * (hn * jax.nn.sigmoid(o.astype(F32)))).astype(BF16)


def _bf16_terms(x):
    hi = x.astype(BF16)
    rest = x - hi.astype(F32)
    mid = rest.astype(BF16)
    return hi, mid, (rest - mid.astype(F32)).astype(BF16)


def _select_sum(sel, terms, dims, sel_is_lhs):
    sel = sel.astype(BF16)
    dot = lambda t: lax.dot_general(*((sel, t) if sel_is_lhs else (t, sel)), (dims, ((), ())),
                                    preferred_element_type=F32)
    return dot(terms[0]) + dot(terms[1]) + dot(terms[2])


def _gate_sums(gates_c, causal, causal_t):
    pick = (lax.broadcasted_iota(jnp.int32, (2 * HEADS, LANES), 0)
            == lax.broadcasted_iota(jnp.int32, (2 * HEADS, LANES), 1))
    terms_c = _bf16_terms(gates_c)
    gates_r = _select_sum(pick, terms_c, ((1,), (1,)), True)
    cum_c = _select_sum(causal, terms_c, ((1,), (0,)), True)
    cum_r = _select_sum(causal_t, _bf16_terms(gates_r), ((1,), (0,)), False)
    return gates_r, cum_c, cum_r


def _mlstm_prompt_chunk(q_ref, k_ref, v_ref, o_ref, gc_ref, hn_ref, y_ref, c_ref, n_ref, m_ref, *, chunk):
    @pl.when(pl.program_id(1) == 0)
    def _():
        c_ref[...] = jnp.zeros(c_ref.shape, F32)
        n_ref[...] = jnp.zeros(n_ref.shape, F32)
        m_ref[...] = jnp.zeros(m_ref.shape, F32)

    head_row = lax.broadcasted_iota(jnp.int32, (HEADS, DK), 0)
    head_lane = lax.broadcasted_iota(jnp.int32, (1, LANES), 1)
    row = lax.broadcasted_iota(jnp.int32, (chunk, chunk), 0)
    col = lax.broadcasted_iota(jnp.int32, (chunk, chunk), 1)
    causal = col <= row
    gates_c = gc_ref[...]
    gates_r, cum_c, cum_r = _gate_sums(gates_c, causal, row <= col)
    n_all = n_ref[0]
    m_all = m_ref[0]
    n_out = jnp.zeros((HEADS, DK), F32)
    m_out = jnp.zeros((1, LANES), F32)

    for hd in range(HEADS):
        q = q_ref[:, hd * DK:(hd + 1) * DK] * BF16(DK ** -0.5)
        k = k_ref[:, hd * DK:(hd + 1) * DK]
        v = v_ref[:, hd * DV:(hd + 1) * DV]
        i_col = gates_c[:, hd:hd + 1]
        i_row = gates_r[hd:hd + 1, :]
        b_col = cum_c[:, HEADS + hd:HEADS + hd + 1]
        b_row = cum_r[HEADS + hd:HEADS + hd + 1, :]
        c_prev = c_ref[0, hd]
        m_prev = m_all[:, hd:hd + 1]

        a = b_col + m_prev
        dm = jnp.where(causal, b_col - b_row + i_row, -jnp.inf)
        m_row = jnp.maximum(a, jnp.max(dm, axis=1, keepdims=True))
        w_intra = jnp.exp(dm - m_row)
        w_inter = jnp.exp(a - m_row)
        s = lax.dot_general(q, k, _NT, preferred_element_type=F32) * w_intra
        num = (w_inter * jnp.dot(q, c_prev.astype(BF16), preferred_element_type=F32)
               + jnp.dot(s.astype(BF16), v, preferred_element_type=F32))
        den = (w_inter * jnp.sum(q.astype(F32) * n_all[hd:hd + 1, :], axis=1, keepdims=True)
               + jnp.sum(s, axis=1, keepdims=True))

        b_last = b_col[chunk - 1:chunk, :]
        m_new = m_row[chunk - 1:chunk, :]
        decay = jnp.exp(b_last + m_prev - m_new)
        kw = k.astype(F32) * jnp.exp(b_last - b_col + i_col - m_new)
        c_ref[0, hd] = decay * c_prev + lax.dot_general(kw.astype(BF16), v, _TN, preferred_element_type=F32)
        n_new = decay * n_all[hd:hd + 1, :] + jnp.sum(kw, axis=0, keepdims=True)
        n_out = jnp.where(head_row == hd, n_new, n_out)
        m_out = jnp.where(head_lane == hd, m_new, m_out)

        y_ref[:, hd * DV:(hd + 1) * DV] = _gated_heads(
            num, den, m_row, o_ref[:, hd * DV:(hd + 1) * DV], hn_ref[:, hd * DV:(hd + 1) * DV])

    n_ref[0] = n_out
    m_ref[0] = m_out


def _mlstm_sample_group(q_ref, k_ref, v_ref, o_ref, gc_ref, hn_ref, c0_ref, n0_ref, m0_ref,
                        y_ref, c_ref, n_ref, m_ref, *, t_len, n_sub):
    n_rows = n_sub * t_len
    row = lax.broadcasted_iota(jnp.int32, (n_rows, n_rows), 0)
    col = lax.broadcasted_iota(jnp.int32, (n_rows, n_rows), 1)
    seq_start = lax.broadcasted_iota(jnp.int32, (n_sub, t_len, n_rows), 0).reshape(n_rows, n_rows) * t_len
    same_seq = (col >= seq_start) & (col < seq_start + t_len)
    causal = same_seq & (col <= row)
    causal_t = same_seq & (row <= col)

    def per_seq(x):
        return jnp.broadcast_to(x, (n_sub, t_len, x.shape[-1])).reshape(n_rows, x.shape[-1])

    def last_token(x):
        return per_seq(x.reshape(n_sub, t_len, x.shape[-1])[:, t_len - 1:t_len, :])

    gates_c = gc_ref[...]
    gates_r, cum_c, cum_r = _gate_sums(gates_c, causal, causal_t)
    m_prev_all = per_seq(m0_ref[...])
    b_last_all = last_token(cum_c)
    head_row = lax.broadcasted_iota(jnp.int32, (HEADS, DK), 0)
    head_lane = lax.broadcasted_iota(jnp.int32, (1, LANES), 1)
    n_out = [jnp.zeros((HEADS, DK), F32)] * n_sub
    m_out = [jnp.zeros((1, LANES), F32)] * n_sub

    for hd in range(HEADS):
        q = q_ref[:, hd * DK:(hd + 1) * DK] * BF16(DK ** -0.5)
        k = k_ref[:, hd * DK:(hd + 1) * DK]
        v = v_ref[:, hd * DV:(hd + 1) * DV]
        i_col = gates_c[:, hd:hd + 1]
        i_row = gates_r[hd:hd + 1, :]
        b_col = cum_c[:, HEADS + hd:HEADS + hd + 1]
        b_row = cum_r[HEADS + hd:HEADS + hd + 1, :]
        m_prev = m_prev_all[:, hd:hd + 1]
        b_last = b_last_all[:, HEADS + hd:HEADS + hd + 1]

        a = b_col + m_prev
        dm = jnp.where(causal, b_col - b_row + i_row, -jnp.inf)
        m_row = jnp.maximum(a, jnp.max(dm, axis=1, keepdims=True))
        w_intra = jnp.exp(dm - m_row)
        w_inter = jnp.exp(a - m_row)
        s = lax.dot_general(q, k, _NT, preferred_element_type=F32) * w_intra
        inter = jnp.concatenate(
            [jnp.dot(q_ref[sub * t_len:(sub + 1) * t_len, hd * DK:(hd + 1) * DK] * BF16(DK ** -0.5),
                     c0_ref[sub, hd].astype(BF16), preferred_element_type=F32)
             for sub in range(n_sub)], axis=0)
        num = w_inter * inter + jnp.dot(s.astype(BF16), v, preferred_element_type=F32)
        n_prev = per_seq(n0_ref[:, hd:hd + 1, :])
        den = (w_inter * jnp.sum(q.astype(F32) * n_prev, axis=1, keepdims=True)
               + jnp.sum(s, axis=1, keepdims=True))
        y_ref[:, hd * DV:(hd + 1) * DV] = _gated_heads(
            num, den, m_row, o_ref[:, hd * DV:(hd + 1) * DV], hn_ref[:, hd * DV:(hd + 1) * DV])

        m_new = last_token(m_row)
        decay = jnp.exp(b_last + m_prev - m_new)
        kw = k.astype(F32) * jnp.exp(b_last - b_col + i_col - m_new)
        for sub in range(n_sub):
            rows = slice(sub * t_len, (sub + 1) * t_len)
            last = slice((sub + 1) * t_len - 1, (sub + 1) * t_len)
            c_ref[sub, hd] = decay[last] * c0_ref[sub, hd] + lax.dot_general(
                kw[rows].astype(BF16), v_ref[rows, hd * DV:(hd + 1) * DV], _TN, preferred_element_type=F32)
            n_new = decay[last] * n0_ref[sub, hd:hd + 1, :] + jnp.sum(kw[rows], axis=0, keepdims=True)
            n_out[sub] = jnp.where(head_row == hd, n_new, n_out[sub])
            m_out[sub] = jnp.where(head_lane == hd, m_new[last], m_out[sub])

    for sub in range(n_sub):
        n_ref[sub] = n_out[sub]
        m_ref[sub] = m_out[sub]


N_PROJ_REFS = 5


def _mlstm_rec_kernel(*refs, chunk, t_len, n_sub, n_carried):
    p_in, s_in = refs[:N_PROJ_REFS], refs[N_PROJ_REFS:2 * N_PROJ_REFS]
    hn_ref, c0_ref, n0_ref, m0_ref = refs[2 * N_PROJ_REFS:2 * N_PROJ_REFS + 4]
    outs = refs[2 * N_PROJ_REFS + 4 + n_carried:]
    yp_ref, ys_ref = outs[:2]
    _mlstm_prompt_chunk(*p_in, hn_ref, yp_ref, *outs[2:5], chunk=chunk)
    _mlstm_sample_group(*s_in, hn_ref, c0_ref, n0_ref, m0_ref, ys_ref, *outs[5:8], t_len=t_len, n_sub=n_sub)


def _mlstm_rec(j, proj, gates_c, head_norm, state_s, new_p, new_s, dims):
    batch, seq, dec_batch, t_len = dims
    chunk, n_sub = REC_CHUNK, REC_GROUP
    n_chunks = seq // chunk
    assert batch * n_chunks == dec_batch // n_sub
    s_blk = n_sub * t_len
    s_base = batch * seq // s_blk

    def proj_specs(blk, base):
        rows = lambda width_blk: (lambda b, c: (base + b * n_chunks + c, width_blk))
        return [pl.BlockSpec((blk, QK_DIM), rows(0)), pl.BlockSpec((blk, QK_DIM), rows(1)),
                pl.BlockSpec((blk, V_DIM), rows(1)), pl.BlockSpec((blk, V_DIM), rows(2)),
                pl.BlockSpec((blk, LANES), rows(0))]

    def state_specs(n_seq_blk, seq_blk_of):
        return [pl.BlockSpec((None, n_seq_blk, HEADS, DK, DV), lambda b, c: (j, seq_blk_of(b, c), 0, 0, 0)),
                pl.BlockSpec((None, n_seq_blk, HEADS, DK), lambda b, c: (j, seq_blk_of(b, c), 0, 0)),
                pl.BlockSpec((None, n_seq_blk, 1, LANES), lambda b, c: (j, seq_blk_of(b, c), 0, 0))]

    def state_shapes(n_seq):
        return [jax.ShapeDtypeStruct((N_MIX, n_seq, HEADS, DK, DV), F32),
                jax.ShapeDtypeStruct((N_MIX, n_seq, HEADS, DK), F32),
                jax.ShapeDtypeStruct((N_MIX, n_seq, 1, LANES), F32)]

    p_state = state_specs(1, lambda b, c: b)
    s_state = state_specs(n_sub, lambda b, c: b * n_chunks + c)
    carried = {}
    if new_p is not None:
        carried.update({2 + i: a for i, a in enumerate(new_p)})
        carried.update({5 + i: a for i, a in enumerate(new_s)})
    proj_ops = (proj, proj, proj, proj, gates_c)
    outs = _call(
        functools.partial(_mlstm_rec_kernel, chunk=chunk, t_len=t_len, n_sub=n_sub, n_carried=len(carried)),
        name="mlstm_rec", grid=(batch, n_chunks), sem=("arbitrary", "arbitrary"),
        in_specs=proj_specs(chunk, 0) + proj_specs(s_blk, s_base) + [_layer_row(j)] + s_state,
        out_specs=[pl.BlockSpec((chunk, V_DIM), lambda b, c: (b * n_chunks + c, 0)),
                   pl.BlockSpec((s_blk, V_DIM), lambda b, c: (b * n_chunks + c, 0))] + p_state + s_state,
        out_shape=[jax.ShapeDtypeStruct((batch * seq, V_DIM), BF16),
                   jax.ShapeDtypeStruct((dec_batch * t_len, V_DIM), BF16)]
                  + state_shapes(batch) + state_shapes(dec_batch),
        operands=proj_ops + proj_ops + (head_norm,) + tuple(state_s),
        carried=carried)
    return outs[0], outs[1], tuple(outs[2:5]), tuple(outs[5:8])


def _out_proj_kernel(x_ref, yp_ref, ys_ref, w_ref, o_ref, *, n_prompt_tiles):
    y = jnp.where(pl.program_id(0) < n_prompt_tiles, yp_ref[...], ys_ref[...])
    o_ref[...] = x_ref[...] + jnp.dot(y, w_ref[...], preferred_element_type=F32)


def _out_proj(h, y_prompt, y_sample, w_bf, n_prompt):
    n_tok = h.shape[0]
    tm = OUT_TM
    npt = n_prompt // tm
    row_spec = pl.BlockSpec((tm, D_MODEL), lambda i: (i, 0))
    return _call(
        functools.partial(_out_proj_kernel, n_prompt_tiles=npt),
        name="mlstm_out", grid=(n_tok // tm,), sem=("arbitrary",),
        in_specs=[row_spec,
                  pl.BlockSpec((tm, V_DIM), lambda i: (jnp.minimum(i, npt - 1), 0)),
                  pl.BlockSpec((tm, V_DIM), lambda i: (jnp.maximum(i - npt, 0), 0)),
                  pl.BlockSpec((V_DIM, D_MODEL), lambda i: (0, 0))],
        out_specs=row_spec, out_shape=jax.ShapeDtypeStruct((n_tok, D_MODEL), F32),
        operands=(h, y_prompt, y_sample, w_bf))


def _mlstm_layer(j, h, u, state_s, new_p, new_s, w_t, wg_t, b_gate, head_norm, w_out, dims):
    proj, gates_c, w_out_bf = _mlstm_in(j, u, w_t, wg_t, b_gate, w_out)
    y_prompt, y_sample, new_p, new_s = _mlstm_rec(j, proj, gates_c, head_norm, state_s, new_p, new_s, dims)
    return _out_proj(h, y_prompt, y_sample, w_out_bf, dims[0] * dims[1]), new_p, new_s


def kernel(x_prompt, x_sample, state_pool, state_mlstm_C, state_mlstm_n, state_mlstm_m, norm_mix, norm_ffn, norm_final, pool_w, pool_scale, mlstm_w_in, mlstm_b_gate, mlstm_head_norm, mlstm_w_out, ffn_w_up, ffn_w_down):
    batch, seq, _ = x_prompt.shape
    dec_batch, t_len, _ = x_sample.shape
    dims = (batch, seq, dec_batch, t_len)
    n_prompt = batch * seq
    x_prompt = x_prompt.reshape(n_prompt, D_MODEL)
    x_sample = x_sample.reshape(dec_batch * t_len, D_MODEL)

    rows3 = lambda a: a.reshape(a.shape[0], 1, a.shape[1])
    norm_mix, norm_ffn, pool_scale, head_norm = map(rows3, (norm_mix, norm_ffn, pool_scale, mlstm_head_norm))
    pad_lanes = lambda a: jnp.pad(a, [(0, 0)] * (a.ndim - 1) + [(0, LANES - a.shape[-1])])
    b_gate = rows3(pad_lanes(mlstm_b_gate))
    state_s = (state_mlstm_C, state_mlstm_n, pad_lanes(state_mlstm_m)[:, :, None, :])
    pool_w_bf = pool_w.astype(BF16)
    w_in_t = jnp.swapaxes(mlstm_w_in, 1, 2)
    w_gate_t = jnp.pad(w_in_t[:, QKVO_DIM:, :], ((0, 0), (0, LANES - 2 * HEADS), (0, 0)))
    ffn_bf = None
    norm_final = norm_final.reshape(1, D_MODEL)

    h = u = new_pool = new_p = new_s = None
    for layer in range(DEPTH):
        j = layer // 2
        if layer % 2 == 0:
            h, new_pool, first_bf = _pool_layer(j, layer, h, x_prompt, x_sample, state_pool, new_pool,
                                                norm_mix, pool_w_bf, pool_scale, ffn_w_up, ffn_w_down, dims)
            ffn_bf = first_bf or ffn_bf
        else:
            h, new_p, new_s = _mlstm_layer(j, h, u, state_s, new_p, new_s, w_in_t, w_gate_t,
                                           b_gate, head_norm, mlstm_w_out, dims)
        h, u, ffn_bf = _ffn_layer(layer, h, norm_ffn, *ffn_bf, ffn_w_up, ffn_w_down, norm_mix, norm_final,
                                  n_prompt)

    y_prompt = h[0].reshape(batch, seq, D_MODEL)
    y_sample = h[1].reshape(dec_batch, t_len, D_MODEL)
    return (y_prompt, y_sample, new_pool[0], new_pool[1],
            new_p[0], new_p[1], new_p[2][:, :, 0, :HEADS],
            new_s[0], new_s[1], new_s[2][:, :, 0, :HEADS])
```
